```python
import math
import jax
import jax.numpy as jnp
from jax import lax
import numpy as np

D_MODEL = 4096
BATCH = 4
SEQ = 2048
DEPTH = 4
DEC_BATCH = 128
DEC_SEQ = 1
PAST_LEN = 8192
PAGE_SIZE = 128

N_MIXERS = 3
N_S5 = (DEPTH + 2) // 3
N_FOX = (DEPTH + 1) // 3
N_MLA = DEPTH // 3
S5_GROUP = 16
S5_GROUPS = D_MODEL // S5_GROUP
S5_STATE = 64
FOX_HEAD_DIM = 128
FOX_HEADS = D_MODEL // FOX_HEAD_DIM
FOX_KV_HEADS = 4
FOX_GQA = FOX_HEADS // FOX_KV_HEADS
FOX_IN = FOX_HEADS * FOX_HEAD_DIM + 2 * FOX_KV_HEADS * FOX_HEAD_DIM + FOX_HEADS
MLA_HEADS = 32
MLA_Q_LORA = 1024
MLA_KV_LORA = 512
MLA_NOPE = 128
MLA_ROPE = 64
MLA_V = 128
MLA_IN = MLA_Q_LORA + MLA_KV_LORA + MLA_ROPE
ROPE_BASE = 10000.0
PEER_KEYS = 128
PEER_EXPERTS = PEER_KEYS * PEER_KEYS
PEER_HEADS = 8
PEER_KEY_DIM = 256
PEER_TOPK = 16
PEER_BLOCK = 128
Q_BLOCK = 128
ALPHA = (2 * DEPTH) ** 0.25
BETA = (8 * DEPTH) ** -0.25
LN_EPS = 1e-5
RMS_EPS = 1e-6
F32 = jnp.float32

kernel_name = 'hybrid_s5_fox_mla_peer_decode_step'


def layer_norm(x, g, b):
    xf = x.astype(F32)
    mu = jnp.mean(xf, axis=-1, keepdims=True)
    var = jnp.mean(jnp.square(xf - mu), axis=-1, keepdims=True)
    return ((xf - mu) * lax.rsqrt(var + LN_EPS) * g.astype(F32) + b.astype(F32)).astype(x.dtype)


def rms_norm(x, g):
    xf = x.astype(F32)
    return (xf * lax.rsqrt(jnp.mean(xf * xf, axis=-1, keepdims=True) + RMS_EPS) * g.astype(F32)).astype(x.dtype)


def rope(x, pos):
    half = x.shape[-1] // 2
    inv = ROPE_BASE ** (-jnp.arange(half, dtype=F32) / half)
    ang = pos[:, None] * inv
    ang = ang.reshape(ang.shape[:1] + (1,) * (x.ndim - 3) + ang.shape[1:])
    cos, sin = jnp.cos(ang), jnp.sin(ang)
    xf = x.astype(F32)
    x1, x2 = xf[..., :half], xf[..., half:]
    return jnp.concatenate([x1 * cos - x2 * sin, x1 * sin + x2 * cos], axis=-1).astype(x.dtype)


def modulation(c, w, b):
    m = (jax.nn.silu(c) @ w + b)[:, None, :]
    return jnp.split(m, 3, axis=-1)


def modulate(x, mod):
    shift, scale, _ = mod
    return (x * (1.0 + scale) + shift).astype(x.dtype)


def post_norm(x, out, mod, g, b):
    gate = mod[2]
    return layer_norm(ALPHA * x + (1.0 + gate) * out, g, b).astype(x.dtype)


def online_softmax_step(carry, s, vals, eq):
    m, l, acc = carry
    m_new = jnp.maximum(m, jnp.max(s, axis=-1))
    corr = jnp.exp(m - m_new)
    p = jnp.exp(s - m_new[..., None])
    return (m_new, l * corr + jnp.sum(p, axis=-1), acc * corr[..., None] + jnp.einsum(eq, p, vals.astype(F32)))


def s5_discretize(lam_re, lam_im, log_step, b_re, b_im):
    lr, li = lam_re.astype(F32), lam_im.astype(F32)
    dt = jnp.exp(log_step.astype(F32))[:, None]
    mag, ang = jnp.exp(lr * dt), li * dt
    a_re, a_im = mag * jnp.cos(ang), mag * jnp.sin(ang)
    den = lr * lr + li * li
    f_re = ((a_re - 1.0) * lr + a_im * li) / den
    f_im = (a_im * lr - (a_re - 1.0) * li) / den
    br, bi = b_re.astype(F32), b_im.astype(F32)
    bb_re = f_re[..., None] * br - f_im[..., None] * bi
    bb_im = f_re[..., None] * bi + f_im[..., None] * br
    return a_re, a_im, bb_re, bb_im


def s5_combine(e1, e2):
    a1r, a1i, b1r, b1i = e1
    a2r, a2i, b2r, b2i = e2
    return (a1r * a2r - a1i * a2i, a1r * a2i + a1i * a2r,
            a2r * b1r - a2i * b1i + b2r, a2r * b1i + a2i * b1r + b2i)


def s5_mixer(h, s0_re, s0_im, w_in, lam_re, lam_im, log_step, b_re, b_im, c_re, c_im, d_skip, w_glu):
    n, t, _ = h.shape
    u = h @ w_in
    ug = u.astype(F32).reshape(n, t, S5_GROUPS, S5_GROUP)
    a_re, a_im, bb_re, bb_im = s5_discretize(lam_re, lam_im, log_step, b_re, b_im)
    bu_re = jnp.einsum('ntgc,gpc->tngp', ug, bb_re)
    bu_im = jnp.einsum('ntgc,gpc->tngp', ug, bb_im)
    s0r, s0i = s0_re.astype(F32), s0_im.astype(F32)
    bu_re = bu_re.at[0].add(a_re * s0r - a_im * s0i)
    bu_im = bu_im.at[0].add(a_re * s0i + a_im * s0r)
    shape = (t, 1) + a_re.shape
    elems = (jnp.broadcast_to(a_re, shape), jnp.broadcast_to(a_im, shape), bu_re, bu_im)
    _, _, s_re, s_im = lax.associative_scan(s5_combine, elems, axis=0)
    y = (jnp.einsum('tngp,gcp->ntgc', s_re, c_re.astype(F32))
         - jnp.einsum('tngp,gcp->ntgc', s_im, c_im.astype(F32)))
    y = y.reshape(n, t, D_MODEL) + d_skip.astype(F32) * u.astype(F32)
    val, gate = jnp.split(jax.nn.gelu(y).astype(w_glu.dtype) @ w_glu, 2, axis=-1)
    return (val * jax.nn.sigmoid(gate)).astype(h.dtype), s_re[-1], s_im[-1]


def fox_project(h, w_in, b_f):
    n, t, _ = h.shape
    z = h @ w_in
    nq, nkv = FOX_HEADS * FOX_HEAD_DIM, FOX_KV_HEADS * FOX_HEAD_DIM
    q = z[..., :nq].reshape(n, t, FOX_KV_HEADS, FOX_GQA, FOX_HEAD_DIM)
    k = z[..., nq:nq + nkv].reshape(n, t, FOX_KV_HEADS, FOX_HEAD_DIM)
    v = z[..., nq + nkv:nq + 2 * nkv].reshape(n, t, FOX_KV_HEADS, FOX_HEAD_DIM)
    logf = jax.nn.log_sigmoid(z[..., nq + 2 * nkv:].astype(F32) + b_f.astype(F32))
    return q, k, v, logf


def fox_prompt_attention(q, k, v, logf):
    n, t = q.shape[:2]
    scale = FOX_HEAD_DIM ** -0.5
    f_cum = jnp.cumsum(logf, axis=1).reshape(n, t, FOX_KV_HEADS, FOX_GQA).transpose(0, 2, 3, 1)
    key_pos = jnp.arange(t)

    def block(i):
        qb = lax.dynamic_slice_in_dim(q, i * Q_BLOCK, Q_BLOCK, axis=1)
        fb = lax.dynamic_slice_in_dim(f_cum, i * Q_BLOCK, Q_BLOCK, axis=3)
        s = jnp.einsum('nqkgd,nskd->nkgqs', qb, k).astype(F32) * scale
        s = s + (fb[..., :, None] - f_cum[..., None, :])
        causal = (i * Q_BLOCK + jnp.arange(Q_BLOCK))[:, None] >= key_pos[None, :]
        p = jax.nn.softmax(jnp.where(causal, s, -jnp.inf), axis=-1)
        return jnp.einsum('nkgqs,nskd->nqkgd', p.astype(v.dtype), v)

    o = lax.map(block, jnp.arange(t // Q_BLOCK))
    return o.transpose(1, 0, 2, 3, 4, 5).reshape(n, t, FOX_HEADS * FOX_HEAD_DIM)


def fox_sample_attention(q, k_new, v_new, logf_new, cache_k, cache_v, cache_logf, layer, page_table):
    n, tn = q.shape[:2]
    n_pages = page_table.shape[1]
    scale = FOX_HEAD_DIM ** -0.5
    lf_past = cache_logf[layer, page_table].astype(F32).reshape(n, n_pages * PAGE_SIZE, FOX_HEADS)
    f_past = jnp.cumsum(lf_past, axis=1)
    r_past = (f_past - f_past[:, -1:]).reshape(n, n_pages, PAGE_SIZE, FOX_KV_HEADS, FOX_GQA)
    r_past = r_past.transpose(1, 0, 3, 4, 2)
    c_new = jnp.cumsum(logf_new, axis=1).reshape(n, tn, FOX_KV_HEADS, FOX_GQA).transpose(0, 2, 3, 1)

    def page_step(carry, xs):
        phys, r_pg = xs
        k_pg = cache_k[layer, phys]
        v_pg = cache_v[layer, phys]
        s = jnp.einsum('nqkgd,nskd->nkgqs', q, k_pg).astype(F32) * scale
        s = s + (c_new[..., :, None] - r_pg[..., None, :])
        return online_softmax_step(carry, s, v_pg, 'nkgqs,nskd->nkgqd'), None

    init = (jnp.full((n, FOX_KV_HEADS, FOX_GQA, tn), -jnp.inf, F32),
            jnp.zeros((n, FOX_KV_HEADS, FOX_GQA, tn), F32),
            jnp.zeros((n, FOX_KV_HEADS, FOX_GQA, tn, FOX_HEAD_DIM), F32))
    carry, _ = lax.scan(page_step, init, (page_table.T, r_past))
    s = jnp.einsum('nqkgd,nskd->nkgqs', q, k_new).astype(F32) * scale
    s = s + (c_new[..., :, None] - c_new[..., None, :])
    s = jnp.where(jnp.tril(jnp.ones((tn, tn), bool)), s, -jnp.inf)
    _, l, acc = online_softmax_step(carry, s, v_new, 'nkgqs,nskd->nkgqd')
    o = (acc / l[..., None]).transpose(0, 3, 1, 2, 4).reshape(n, tn, FOX_HEADS * FOX_HEAD_DIM)
    return o.astype(q.dtype)


def mla_project(h, w_in, q_norm, kv_norm, w_uq, pos):
    n, t, _ = h.shape
    z = h @ w_in
    cq = rms_norm(z[..., :MLA_Q_LORA], q_norm)
    ckv = rms_norm(z[..., MLA_Q_LORA:MLA_Q_LORA + MLA_KV_LORA], kv_norm)
    kpe = rope(z[..., MLA_Q_LORA + MLA_KV_LORA:], pos)
    q = (cq @ w_uq).reshape(n, t, MLA_HEADS, MLA_NOPE + MLA_ROPE)
    return q[..., :MLA_NOPE], rope(q[..., MLA_NOPE:], pos), ckv, kpe


def mla_prompt_attention(q_nope, q_pe, ckv, kpe, w_ukv):
    n, t = q_nope.shape[:2]
    scale = (MLA_NOPE + MLA_ROPE) ** -0.5
    k_nope = jnp.einsum('nsc,chd->nshd', ckv, w_ukv[..., :MLA_NOPE])
    v = jnp.einsum('nsc,chd->nshd', ckv, w_ukv[..., MLA_NOPE:])
    key_pos = jnp.arange(t)

    def block(i):
        qn = lax.dynamic_slice_in_dim(q_nope, i * Q_BLOCK, Q_BLOCK, axis=1)
        qr = lax.dynamic_slice_in_dim(q_pe, i * Q_BLOCK, Q_BLOCK, axis=1)
        s = (jnp.einsum('nqhd,nshd->nhqs', qn, k_nope)
             + jnp.einsum('nqhr,nsr->nhqs', qr, kpe)).astype(F32) * scale
        causal = (i * Q_BLOCK + jnp.arange(Q_BLOCK))[:, None] >= key_pos[None, :]
        p = jax.nn.softmax(jnp.where(causal, s, -jnp.inf), axis=-1)
        return jnp.einsum('nhqs,nshd->nqhd', p.astype(v.dtype), v)

    o = lax.map(block, jnp.arange(t // Q_BLOCK))
    return o.transpose(1, 0, 2, 3, 4).reshape(n, t, MLA_HEADS * MLA_V)


def mla_sample_attention(q_nope, q_pe, ckv_new, kpe_new, w_ukv, cache_ckv, cache_kpe, layer, page_table):
    n, tn = q_nope.shape[:2]
    scale = (MLA_NOPE + MLA_ROPE) ** -0.5
    q_lat = jnp.einsum('nqhd,chd->nhqc', q_nope, w_ukv[..., :MLA_NOPE])
    q_rot = q_pe.transpose(0, 2, 1, 3)

    def page_step(carry, phys):
        c_pg = cache_ckv[layer, phys]
        r_pg = cache_kpe[layer, phys]
        s = (jnp.einsum('nhqc,nsc->nhqs', q_lat, c_pg)
             + jnp.einsum('nhqr,nsr->nhqs', q_rot, r_pg)).astype(F32) * scale
        return online_softmax_step(carry, s, c_pg, 'nhqs,nsc->nhqc'), None

    init = (jnp.full((n, MLA_HEADS, tn), -jnp.inf, F32),
            jnp.zeros((n, MLA_HEADS, tn), F32),
            jnp.zeros((n, MLA_HEADS, tn, MLA_KV_LORA), F32))
    carry, _ = lax.scan(page_step, init, page_table.T)
    s = (jnp.einsum('nhqc,nsc->nhqs', q_lat, ckv_new)
         + jnp.einsum('nhqr,nsr->nhqs', q_rot, kpe_new)).astype(F32) * scale
    s = jnp.where(jnp.tril(jnp.ones((tn, tn), bool)), s, -jnp.inf)
    _, l, acc = online_softmax_step(carry, s, ckv_new, 'nhqs,nsc->nhqc')
    o = jnp.einsum('nhqc,chd->nqhd', acc / l[..., None], w_ukv[..., MLA_NOPE:].astype(F32))
    return o.reshape(n, tn, MLA_HEADS * MLA_V).astype(q_nope.dtype)


def peer_ffn(h, w_q, subkeys, u_tab, v_tab):
    n, t, d = h.shape
    m = n * t
    m_pad = -(-m // PEER_BLOCK) * PEER_BLOCK
    xb = jnp.pad(h.reshape(m, d), ((0, m_pad - m), (0, 0))).reshape(m_pad // PEER_BLOCK, PEER_BLOCK, d)
    sk = subkeys.astype(F32)

    def block(xt):
        q = (xt @ w_q).astype(F32).reshape(PEER_BLOCK, PEER_HEADS, 2, PEER_KEY_DIM // 2)
        s1 = jnp.einsum('thd,hkd->thk', q[:, :, 0], sk[0])
        s2 = jnp.einsum('thd,hkd->thk', q[:, :, 1], sk[1])
        v1, i1 = lax.top_k(s1, PEER_TOPK)
        v2, i2 = lax.top_k(s2, PEER_TOPK)
        cand = (v1[..., :, None] + v2[..., None, :]).reshape(PEER_BLOCK, PEER_HEADS, PEER_TOPK * PEER_TOPK)
        cidx = (i1[..., :, None] * PEER_KEYS + i2[..., None, :]).reshape(PEER_BLOCK, PEER_HEADS, PEER_TOPK * PEER_TOPK)
        top, sel = lax.top_k(cand, PEER_TOPK)
        eidx = jnp.take_along_axis(cidx, sel, axis=-1)
        g = jax.nn.softmax(top, axis=-1)
        act = jax.nn.gelu(jnp.einsum('td,thkd->thk', xt, u_tab[eidx]).astype(F32))
        return jnp.einsum('thk,thkd->td', (g * act).astype(v_tab.dtype), v_tab[eidx])

    out = lax.map(block, xb).reshape(m_pad, d)[:m]
    return out.reshape(n, t, d).astype(h.dtype)


def setup_inputs(seed: int = 0) -> dict:
    key = jax.random.key(seed)
    ks = jax.random.split(key, 48)
    n_pages = PAST_LEN // PAGE_SIZE
    n_pool = (DEC_BATCH * n_pages * 5) // 4

    def nrm(i, shape, scale=1.0):
        return scale * jax.random.normal(ks[i], shape, F32)

    page_table = jax.random.permutation(ks[0], n_pool)[:DEC_BATCH * n_pages].reshape(DEC_BATCH, n_pages).astype(jnp.int32)
    lam_im = jnp.pi * jnp.arange(S5_STATE, dtype=F32)
    return {
        'x_prompt': nrm(1, (BATCH, SEQ, D_MODEL)),
        'x_sample': nrm(2, (DEC_BATCH, DEC_SEQ, D_MODEL)),
        'state_s5_re': nrm(3, (N_S5, DEC_BATCH, S5_GROUPS, S5_STATE), 0.1),
        'state_s5_im': nrm(4, (N_S5, DEC_BATCH, S5_GROUPS, S5_STATE), 0.1),
        'cache_fox_k': nrm(5, (N_FOX, n_pool, PAGE_SIZE, FOX_KV_HEADS, FOX_HEAD_DIM)),
        'cache_fox_v': nrm(6, (N_FOX, n_pool, PAGE_SIZE, FOX_KV_HEADS, FOX_HEAD_DIM)),
        'cache_fox_logf': jax.nn.log_sigmoid(jax.random.uniform(ks[7], (N_FOX, n_pool, PAGE_SIZE, FOX_HEADS), F32, 1.0, 6.0)),
        'cache_mla_ckv': nrm(8, (N_MLA, n_pool, PAGE_SIZE, MLA_KV_LORA)),
        'cache_mla_kpe': nrm(9, (N_MLA, n_pool, PAGE_SIZE, MLA_ROPE)),
        'page_table': page_table,
        'c_prompt': nrm(10, (BATCH, D_MODEL)),
        'c_sample': nrm(11, (DEC_BATCH, D_MODEL)),
        'ada_w': nrm(12, (DEPTH, 2, D_MODEL, 3 * D_MODEL), 0.1 * D_MODEL ** -0.5),
        'ada_b': nrm(13, (DEPTH, 2, 3 * D_MODEL), 0.01),
        'ln_g': 1.0 + nrm(14, (DEPTH, 2, D_MODEL), 0.02),
        'ln_b': nrm(15, (DEPTH, 2, D_MODEL), 0.02),
        's5_w_in': nrm(16, (N_S5, D_MODEL, D_MODEL), D_MODEL ** -0.5),
        's5_lam_re': -0.5 + nrm(17, (N_S5, S5_GROUPS, S5_STATE), 0.01),
        's5_lam_im': lam_im + nrm(18, (N_S5, S5_GROUPS, S5_STATE), 0.01),
        's5_log_step': jax.random.uniform(ks[19], (N_S5, S5_GROUPS), F32, math.log(1e-3), math.log(1e-1)),
        's5_b_re': nrm(20, (N_S5, S5_GROUPS, S5_STATE, S5_GROUP), (2 * S5_GROUP) ** -0.5),
        's5_b_im': nrm(21, (N_S5, S5_GROUPS, S5_STATE, S5_GROUP), (2 * S5_GROUP) ** -0.5),
        's5_c_re': nrm(22, (N_S5, S5_GROUPS, S5_GROUP, S5_STATE), S5_STATE ** -0.5),
        's5_c_im': nrm(23, (N_S5, S5_GROUPS, S5_GROUP, S5_STATE), S5_STATE ** -0.5),
        's5_d': nrm(24, (N_S5, D_MODEL)),
        's5_w_glu': nrm(25, (N_S5, D_MODEL, 2 * D_MODEL), BETA * D_MODEL ** -0.5),
        'fox_w_in': nrm(26, (N_FOX, D_MODEL, FOX_IN), D_MODEL ** -0.5),
        'fox_b_f': jax.random.uniform(ks[27], (N_FOX, FOX_HEADS), F32, 1.0, 6.0),
        'fox_w_o': nrm(28, (N_FOX, FOX_HEADS * FOX_HEAD_DIM, D_MODEL), BETA * (FOX_HEADS * FOX_HEAD_DIM) ** -0.5),
        'mla_w_in': nrm(29, (N_MLA, D_MODEL, MLA_IN), D_MODEL ** -0.5),
        'mla_q_norm': 1.0 + nrm(30, (N_MLA, MLA_Q_LORA), 0.02),
        'mla_kv_norm': 1.0 + nrm(31, (N_MLA, MLA_KV_LORA), 0.02),
        'mla_w_uq': nrm(32, (N_MLA, MLA_Q_LORA, MLA_HEADS * (MLA_NOPE + MLA_ROPE)), MLA_Q_LORA ** -0.5),
        'mla_w_ukv': nrm(33, (N_MLA, MLA_KV_LORA, MLA_HEADS * (MLA_NOPE + MLA_V)), MLA_KV_LORA ** -0.5),
        'mla_w_o': nrm(34, (N_MLA, MLA_HEADS * MLA_V, D_MODEL), BETA * (MLA_HEADS * MLA_V) ** -0.5),
        'peer_w_q': nrm(35, (DEPTH, D_MODEL, PEER_HEADS * PEER_KEY_DIM), D_MODEL ** -0.5),
        'peer_subkeys': nrm(36, (DEPTH, 2, PEER_HEADS, PEER_KEYS, PEER_KEY_DIM // 2), (PEER_KEY_DIM // 2) ** -0.5),
        'peer_u': nrm(37, (DEPTH, PEER_EXPERTS, D_MODEL), D_MODEL ** -0.5),
        'peer_v': nrm(38, (DEPTH, PEER_EXPERTS, D_MODEL), BETA * PEER_HEADS ** -0.5),
    }


def reference(x_prompt, x_sample, state_s5_re, state_s5_im, cache_fox_k, cache_fox_v, cache_fox_logf,
              cache_mla_ckv, cache_mla_kpe, page_table, c_prompt, c_sample,
              ada_w, ada_b, ln_g, ln_b,
              s5_w_in, s5_lam_re, s5_lam_im, s5_log_step, s5_b_re, s5_b_im, s5_c_re, s5_c_im, s5_d, s5_w_glu,
              fox_w_in, fox_b_f, fox_w_o,
              mla_w_in, mla_q_norm, mla_kv_norm, mla_w_uq, mla_w_ukv, mla_w_o,
              peer_w_q, peer_subkeys, peer_u, peer_v):
    n_p, t_p = x_prompt.shape[:2]
    n_s, t_s = x_sample.shape[:2]
    pos_p = jnp.arange(t_p, dtype=F32)
    pos_s = PAST_LEN + jnp.arange(t_s, dtype=F32)
    yp, ys = x_prompt, x_sample
    s5r_p, s5i_p, s5r_s, s5i_s = [], [], [], []
    fk_p, fv_p, fl_p, fk_s, fv_s, fl_s = [], [], [], [], [], []
    ckv_p, kpe_p, ckv_s, kpe_s = [], [], [], []
    for i in range(DEPTH):
        kind, j = i % N_MIXERS, i // N_MIXERS
        mod_p = modulation(c_prompt, ada_w[i, 0], ada_b[i, 0])
        mod_s = modulation(c_sample, ada_w[i, 0], ada_b[i, 0])
        hp, hs = modulate(yp, mod_p), modulate(ys, mod_s)
        if kind == 0:
            prm = (s5_w_in[j], s5_lam_re[j], s5_lam_im[j], s5_log_step[j], s5_b_re[j], s5_b_im[j],
                   s5_c_re[j], s5_c_im[j], s5_d[j], s5_w_glu[j])
            zero = jnp.zeros((n_p, S5_GROUPS, S5_STATE), F32)
            op, sr, si = s5_mixer(hp, zero, zero, *prm)
            s5r_p.append(sr)
            s5i_p.append(si)
            os_, sr, si = s5_mixer(hs, state_s5_re[j], state_s5_im[j], *prm)
            s5r_s.append(sr)
            s5i_s.append(si)
        elif kind == 1:
            q, k, v, lf = fox_project(hp, fox_w_in[j], fox_b_f[j])
            op = fox_prompt_attention(q, k, v, lf) @ fox_w_o[j]
            fk_p.append(k)
            fv_p.append(v)
            fl_p.append(lf)
            q, k, v, lf = fox_project(hs, fox_w_in[j], fox_b_f[j])
            os_ = fox_sample_attention(q, k, v, lf, cache_fox_k, cache_fox_v, cache_fox_logf, j, page_table) @ fox_w_o[j]
            fk_s.append(k)
            fv_s.append(v)
            fl_s.append(lf)
        else:
            w_ukv = mla_w_ukv[j].reshape(MLA_KV_LORA, MLA_HEADS, MLA_NOPE + MLA_V)
            qn, qr, ckv, kpe = mla_project(hp, mla_w_in[j], mla_q_norm[j], mla_kv_norm[j], mla_w_uq[j], pos_p)
            op = mla_prompt_attention(qn, qr, ckv, kpe, w_ukv) @ mla_w_o[j]
            ckv_p.append(ckv)
            kpe_p.append(kpe)
            qn, qr, ckv, kpe = mla_project(hs, mla_w_in[j], mla_q_norm[j], mla_kv_norm[j], mla_w_uq[j], pos_s)
            os_ = mla_sample_attention(qn, qr, ckv, kpe, w_ukv, cache_mla_ckv, cache_mla_kpe, j, page_table) @ mla_w_o[j]
            ckv_s.append(ckv)
            kpe_s.append(kpe)
        yp = post_norm(yp, op, mod_p, ln_g[i, 0], ln_b[i, 0])
        ys = post_norm(ys, os_, mod_s, ln_g[i, 0], ln_b[i, 0])
        mod_p = modulation(c_prompt, ada_w[i, 1], ada_b[i, 1])
        mod_s = modulation(c_sample, ada_w[i, 1], ada_b[i, 1])
        fp = peer_ffn(modulate(yp, mod_p), peer_w_q[i], peer_subkeys[i], peer_u[i], peer_v[i])
        fs = peer_ffn(modulate(ys, mod_s), peer_w_q[i], peer_subkeys[i], peer_u[i], peer_v[i])
        yp = post_norm(yp, fp, mod_p, ln_g[i, 1], ln_b[i, 1])
        ys = post_norm(ys, fs, mod_s, ln_g[i, 1], ln_b[i, 1])
    return (yp, ys,
            jnp.stack(s5r_p), jnp.stack(s5i_p), jnp.stack(fk_p), jnp.stack(fv_p), jnp.stack(fl_p),
            jnp.stack(ckv_p), jnp.stack(kpe_p),
            jnp.stack(s5r_s), jnp.stack(s5i_s), jnp.stack(fk_s), jnp.stack(fv_s), jnp.stack(fl_s),
            jnp.stack(ckv_s), jnp.stack(kpe_s))
```

```python
import functools
import math

import jax
import jax.numpy as jnp
from jax import lax
from jax.experimental import pallas as pl
from jax.experimental.pallas import tpu as pltpu

F32 = jnp.float32
BF16 = jnp.bfloat16
HI = lax.Precision.HIGHEST

LANE = 128
VMEM_LIMIT = 56 * 1024 * 1024
LN_EPS = 1e-5
RMS_EPS = 1e-6
ROPE_BASE = 10000.0
S5_CHUNK = 8
PEER_TOPK = 16
NT = (((1,), (1,)), ((), ()))


def _cp(*sem):
    return pltpu.CompilerParams(dimension_semantics=sem, vmem_limit_bytes=VMEM_LIMIT)


def _pick(n, cands):
    for c in cands:
        if c <= n and n % c == 0:
            return c
    return n


def _mm_kernel(x_ref, w_ref, *o_refs):
    acc = jnp.dot(x_ref[...], w_ref[...], preferred_element_type=F32)
    for o_ref in o_refs:
        o_ref[...] = acc.astype(o_ref.dtype)


def _mm(x, w, out_dtypes=(F32,), tm=1024, tn=1024):
    m, k = x.shape
    n = w.shape[1]
    tm = _pick(m, (tm, 512, 256, 128))
    tn = _pick(n, (tn, 512, 256, 128))
    outs = pl.pallas_call(
        _mm_kernel,
        grid=(m // tm, n // tn),
        in_specs=[pl.BlockSpec((tm, k), lambda i, j: (i, 0)),
                  pl.BlockSpec((k, tn), lambda i, j: (0, j))],
        out_specs=[pl.BlockSpec((tm, tn), lambda i, j: (i, j)) for _ in out_dtypes],
        out_shape=[jax.ShapeDtypeStruct((m, n), d) for d in out_dtypes],
        compiler_params=_cp("parallel", "arbitrary"),
        name="mm",
    )(x, w)
    return outs[0] if len(out_dtypes) == 1 else outs


def _glu_kernel(x_ref, wv_ref, wg_ref, o_ref):
    x = x_ref[...]
    val = jnp.dot(x, wv_ref[...], preferred_element_type=F32)
    gate = jnp.dot(x, wg_ref[...], preferred_element_type=F32)
    o_ref[...] = val * jax.nn.sigmoid(gate)


def _mm_glu(x, w):
    m, k = x.shape
    n = w.shape[1] // 2
    tm = _pick(m, (1024, 512, 256, 128))
    tn = _pick(n, (512, 256, 128))
    nb = n // tn
    return pl.pallas_call(
        _glu_kernel,
        grid=(m // tm, nb),
        in_specs=[pl.BlockSpec((tm, k), lambda i, j: (i, 0)),
                  pl.BlockSpec((k, tn), lambda i, j: (0, j)),
                  pl.BlockSpec((k, tn), lambda i, j: (0, j + nb))],
        out_specs=pl.BlockSpec((tm, tn), lambda i, j: (i, j)),
        out_shape=jax.ShapeDtypeStruct((m, n), F32),
        compiler_params=_cp("parallel", "arbitrary"),
        name="mm_glu",
    )(x, w, w)


def _logsig_kernel(x_ref, w_ref, b_ref, o_ref):
    z = jnp.dot(x_ref[...], w_ref[...], preferred_element_type=F32) + b_ref[...]
    o_ref[...] = jnp.minimum(z, 0.0) - jnp.log1p(jnp.exp(-jnp.abs(z)))


def _mm_logsig(x, w, b):
    m, k = x.shape
    n = w.shape[1]
    tm = _pick(m, (1024, 512, 256, 128))
    return pl.pallas_call(
        _logsig_kernel,
        grid=(m // tm,),
        in_specs=[pl.BlockSpec((tm, k), lambda i: (i, 0)),
                  pl.BlockSpec((k, n), lambda i: (0, 0)),
                  pl.BlockSpec((1, n), lambda i: (0, 0))],
        out_specs=pl.BlockSpec((tm, n), lambda i: (i, 0)),
        out_shape=jax.ShapeDtypeStruct((m, n), F32),
        compiler_params=_cp("parallel"),
        name="mm_logsig",
    )(x, w, b.reshape(1, n).astype(F32))


def _rope_kernel(x_ref, w_ref, wr_ref, cos_ref, sin_ref, *o_refs):
    x = x_ref[...]
    a = jnp.dot(x, w_ref[...], preferred_element_type=F32)
    b = jnp.dot(x, wr_ref[...], preferred_element_type=F32)
    reps = a.shape[1] // cos_ref.shape[1]
    cos, sin = cos_ref[...], sin_ref[...]
    if reps > 1:
        cos = jnp.concatenate([cos] * reps, axis=1)
        sin = jnp.concatenate([sin] * reps, axis=1)
    out = a * cos + b * sin
    for o_ref in o_refs:
        o_ref[...] = out.astype(o_ref.dtype)


def _mm_rope(x, w, w_rot, cos, sin, out_dtypes):
    m, k = x.shape
    n = w.shape[1]
    period = cos.shape[1]
    tm = _pick(m, (1024, 512, 256, 128))
    tn = _pick(n, (1024, 512, 256, 128)) if n % period == 0 and period % LANE == 0 else n
    outs = pl.pallas_call(
        _rope_kernel,
        grid=(m // tm, n // tn),
        in_specs=[pl.BlockSpec((tm, k), lambda i, j: (i, 0)),
                  pl.BlockSpec((k, tn), lambda i, j: (0, j)),
                  pl.BlockSpec((k, tn), lambda i, j: (0, j)),
                  pl.BlockSpec((tm, period), lambda i, j: (i, 0)),
                  pl.BlockSpec((tm, period), lambda i, j: (i, 0))],
        out_specs=[pl.BlockSpec((tm, tn), lambda i, j: (i, j)) for _ in out_dtypes],
        out_shape=[jax.ShapeDtypeStruct((m, n), d) for d in out_dtypes],
        compiler_params=_cp("parallel", "arbitrary"),
        name="mm_rope",
    )(x, w, w_rot, cos, sin)
    return outs[0] if len(out_dtypes) == 1 else outs


def _headmm_kernel(x_ref, w_ref, o_ref, *, trans_w):
    if trans_w:
        acc = lax.dot_general(x_ref[...], w_ref[...], NT, preferred_element_type=F32)
    else:
        acc = jnp.dot(x_ref[...], w_ref[...], preferred_element_type=F32)
    o_ref[...] = acc.astype(o_ref.dtype)


def _head_mm(x, w, heads, w_off, trans_w, out_dtype):
    m = x.shape[0]
    dx = x.shape[1] // heads
    r = w.shape[0]
    dw = w.shape[1] // (2 * heads)
    do = r if trans_w else dw
    return pl.pallas_call(
        functools.partial(_headmm_kernel, trans_w=trans_w),
        grid=(heads,),
        in_specs=[pl.BlockSpec((m, dx), lambda h: (0, h)),
                  pl.BlockSpec((r, dw), lambda h: (0, 2 * h + w_off))],
        out_specs=pl.BlockSpec((m, do), lambda h: (0, h)),
        out_shape=jax.ShapeDtypeStruct((m, heads * do), out_dtype),
        compiler_params=_cp("parallel"),
        name="head_mm",
    )(x, w)


def _mod_kernel(c_ref, w_ref, b_ref, o_ref):
    c = c_ref[...]
    a = (c * jax.nn.sigmoid(c)).astype(BF16)
    o_ref[0] = jnp.dot(a, w_ref[0].astype(BF16), preferred_element_type=F32) + b_ref[0]


def _modulation_all(c_all, ada_w, ada_b):
    r, d = c_all.shape
    depth = ada_w.shape[0]
    s = depth * 2
    n3 = ada_w.shape[-1]
    w = ada_w.reshape(s, d, n3)
    b = ada_b.reshape(s, 1, n3)
    tn = _pick(n3, (512, 256, 128))
    return pl.pallas_call(
        _mod_kernel,
        grid=(s, n3 // tn),
        in_specs=[pl.BlockSpec((r, d), lambda i, j: (0, 0)),
                  pl.BlockSpec((1, d, tn), lambda i, j: (i, 0, j)),
                  pl.BlockSpec((1, 1, tn), lambda i, j: (i, 0, j))],
        out_specs=pl.BlockSpec((1, r, tn), lambda i, j: (i, 0, j)),
        out_shape=jax.ShapeDtypeStruct((s, r, n3), F32),
        compiler_params=_cp("parallel", "arbitrary"),
        name="modulation",
    )(c_all, w, b)


def _ln_mod_kernel(*refs, alpha, do_ln, do_mod):
    it = iter(refs)
    x_ref = next(it)
    if do_ln:
        f_ref, gate_ref, g_ref, b_ref = next(it), next(it), next(it), next(it)
    if do_mod:
        scale_ref, shift_ref = next(it), next(it)
    if do_ln:
        y_ref = next(it)
    if do_mod:
        h_ref = next(it)
    y = x_ref[0]
    if do_ln:
        xf = alpha * y + (1.0 + gate_ref[0]) * f_ref[0]
        mu = jnp.mean(xf, axis=-1, keepdims=True)
        xc = xf - mu
        var = jnp.mean(xc * xc, axis=-1, keepdims=True)
        y = xc * lax.rsqrt(var + LN_EPS) * g_ref[...] + b_ref[...]
        y_ref[0] = y
    if do_mod:
        h_ref[0] = (y * (1.0 + scale_ref[0]) + shift_ref[0]).astype(BF16)


def _ln_mod(x, f=None, gate=None, g=None, b=None, scale=None, shift=None, *, alpha):
    n, t, d = x.shape
    do_ln, do_mod = f is not None, scale is not None
    tt = _pick(t, (256, 128))
    xspec = pl.BlockSpec((1, tt, d), lambda i, j: (i, j, 0))

    def mspec(a):
        if a.shape[1] == 1:
            return pl.BlockSpec((1, 1, d), lambda i, j: (i, 0, 0))
        return xspec

    vspec = pl.BlockSpec((1, d), lambda i, j: (0, 0))
    args, specs, oshapes, ospecs = [x], [xspec], [], []
    if do_ln:
        args += [f, gate, g.reshape(1, d), b.reshape(1, d)]
        specs += [xspec, mspec(gate), vspec, vspec]
        oshapes.append(jax.ShapeDtypeStruct((n, t, d), F32))
        ospecs.append(xspec)
    if do_mod:
        args += [scale, shift]
        specs += [mspec(scale), mspec(shift)]
        oshapes.append(jax.ShapeDtypeStruct((n, t, d), BF16))
        ospecs.append(xspec)
    outs = pl.pallas_call(
        functools.partial(_ln_mod_kernel, alpha=alpha, do_ln=do_ln, do_mod=do_mod),
        grid=(n, t // tt),
        in_specs=specs, out_specs=ospecs, out_shape=oshapes,
        compiler_params=_cp("parallel", "parallel"),
        name="ln_mod",
    )(*args)
    return outs if len(outs) > 1 else outs[0]


def _s5_discretize(lam_re, lam_im, log_step, b_re, b_im):
    lr, li = lam_re.astype(F32), lam_im.astype(F32)
    dt = jnp.exp(log_step.astype(F32))[:, None]
    mag, ang = jnp.exp(lr * dt), li * dt
    a_re, a_im = mag * jnp.cos(ang), mag * jnp.sin(ang)
    den = lr * lr + li * li
    f_re = ((a_re - 1.0) * lr + a_im * li) / den
    f_im = (a_im * lr - (a_re - 1.0) * li) / den
    br, bi = b_re.astype(F32), b_im.astype(F32)
    bb_re = f_re[..., None] * br - f_im[..., None] * bi
    bb_im = f_re[..., None] * bi + f_im[..., None] * br
    return lr * dt, li * dt, a_re, a_im, bb_re, bb_im


def _s5_powers(ldt_re, ldt_im, ks):
    k = jnp.asarray(ks, F32)[:, None, None]
    mag, ang = jnp.exp(ldt_re * k), ldt_im * k
    return mag * jnp.cos(ang), mag * jnp.sin(ang)


def _block_diag(x, gl):
    *lead, g, a, b = x.shape
    q = g // gl
    xq = x.reshape(*lead, q, gl, a, b)
    eye = jnp.eye(gl, dtype=x.dtype)
    out = xq[..., :, :, :, None, :] * eye[:, None, :, None]
    return out.reshape(*lead, q, gl * a, gl * b)


def _s5_operators(prm, chunk, n_chunks):
    lam_re, lam_im, log_step, b_re, b_im, c_re, c_im = prm
    g, p = lam_re.shape
    c = b_re.shape[-1]
    gl = LANE // c
    q = g // gl
    ldr, ldi, a_re, a_im, bb_re, bb_im = _s5_discretize(lam_re, lam_im, log_step, b_re, b_im)
    cr, ci = c_re.astype(F32), c_im.astype(F32)
    ak_re, ak_im = _s5_powers(ldr, ldi, list(range(chunk + 1)))
    m_re = ak_re[..., None] * bb_re - ak_im[..., None] * bb_im
    m_im = ak_re[..., None] * bb_im + ak_im[..., None] * bb_re
    kk = (jnp.einsum("gop,kgpi->kgio", cr, m_re[:chunk], precision=HI)
          - jnp.einsum("gop,kgpi->kgio", ci, m_im[:chunk], precision=HI))
    bd = _block_diag(kk, gl)
    lag = jnp.arange(chunk)[None, :] - jnp.arange(chunk)[:, None]
    tsel = bd[jnp.clip(lag, 0, chunk - 1)] * (lag >= 0)[:, :, None, None, None]
    t_op = tsel.transpose(2, 0, 3, 1, 4).reshape(q, chunk * LANE, chunk * LANE).astype(BF16)
    rev = jnp.arange(chunk - 1, -1, -1)
    w_re = _block_diag(m_re[rev].transpose(0, 1, 3, 2), gl)
    w_im = _block_diag(m_im[rev].transpose(0, 1, 3, 2), gl)
    w_op = jnp.concatenate([w_re, w_im], axis=-1).transpose(1, 0, 2, 3)
    w_op = w_op.reshape(q, chunk * LANE, 2 * gl * p).astype(BF16)
    pr, pi = ak_re[1:], ak_im[1:]
    v_re = cr[None] * pr[:, :, None, :] - ci[None] * pi[:, :, None, :]
    v_im = -cr[None] * pi[:, :, None, :] - ci[None] * pr[:, :, None, :]
    v_re = _block_diag(v_re.transpose(0, 1, 3, 2), gl)
    v_im = _block_diag(v_im.transpose(0, 1, 3, 2), gl)
    v_op = jnp.concatenate([v_re, v_im], axis=2).transpose(1, 2, 0, 3)
    v_op = v_op.reshape(q, 2 * gl * p, chunk * LANE).astype(BF16)
    nsteps = max(1, int(math.ceil(math.log2(max(n_chunks, 2)))))
    ad_re, ad_im = _s5_powers(ldr, ldi, [chunk * (1 << i) for i in range(nsteps)])
    ad = jnp.concatenate([ad_re.reshape(nsteps, q, gl * p), ad_im.reshape(nsteps, q, gl * p)], axis=-1)
    rows = -(-nsteps // 8) * 8
    ad = jnp.pad(ad.transpose(1, 0, 2), ((0, 0), (0, rows - nsteps), (0, 0)))
    return t_op, w_op, v_op, ad, nsteps


def _s5_step_operators(prm):
    lam_re, lam_im, log_step, b_re, b_im, c_re, c_im = prm
    g, p = lam_re.shape
    gl = LANE // b_re.shape[-1]
    q = g // gl
    _, _, a_re, a_im, bb_re, bb_im = _s5_discretize(lam_re, lam_im, log_step, b_re, b_im)
    bb = jnp.concatenate([_block_diag(bb_re.transpose(0, 2, 1), gl),
                          _block_diag(bb_im.transpose(0, 2, 1), gl)], axis=-1).astype(BF16)
    cr, ci = c_re.astype(F32), c_im.astype(F32)
    cc = jnp.concatenate([_block_diag(cr.transpose(0, 2, 1), gl),
                          _block_diag(-ci.transpose(0, 2, 1), gl)], axis=1).astype(BF16)
    a = jnp.concatenate([a_re.reshape(q, 1, gl * p), a_im.reshape(q, 1, gl * p)], axis=-1)
    return bb, cc, a


def _s5_scan_kernel(u_ref, t_ref, w_ref, v_ref, ad_ref, d_ref, y_ref, sfin_ref, *, n, nk, nsteps):
    u = u_ref[0]
    ub = u.astype(BF16)
    x = jnp.dot(ub, w_ref[0], preferred_element_type=F32)
    half = x.shape[1] // 2
    xr, xi = x[:, :half], x[:, half:]
    rows = x.shape[0]
    k = lax.rem(lax.broadcasted_iota(jnp.int32, (rows, 1), 0), nk)
    for i in range(nsteps):
        d = 1 << i
        if d >= nk:
            break
        ar, ai = ad_ref[0, i:i + 1, :half], ad_ref[0, i:i + 1, half:]
        keep = k >= d
        sr = jnp.where(keep, pltpu.roll(xr, d, 0), 0.0)
        si = jnp.where(keep, pltpu.roll(xi, d, 0), 0.0)
        xr, xi = xr + (ar * sr - ai * si), xi + (ar * si + ai * sr)
    keep = k >= 1
    s0 = jnp.concatenate([jnp.where(keep, pltpu.roll(xr, 1, 0), 0.0),
                          jnp.where(keep, pltpu.roll(xi, 1, 0), 0.0)], axis=1).astype(BF16)
    y = jnp.dot(ub, t_ref[0], preferred_element_type=F32)
    y = y + jnp.dot(s0, v_ref[0], preferred_element_type=F32)
    y = y + d_ref[0] * u
    y_ref[0] = jax.nn.gelu(y).astype(BF16)
    for s in range(n):
        last = (s + 1) * nk - 1
        sfin_ref[0, s:s + 1, :] = jnp.concatenate([xr[last:last + 1], xi[last:last + 1]], axis=1)


def _s5_prompt(h, prm, w_in, d_skip, w_glu):
    n, t, dm = h.shape
    g, p = prm[0].shape
    c = prm[3].shape[-1]
    gl = LANE // c
    q = g // gl
    chunk = S5_CHUNK
    nk = t // chunk
    u = _mm(h.reshape(n * t, dm), w_in)
    uc = u.reshape(n, nk, chunk, q, LANE).transpose(3, 0, 1, 2, 4).reshape(q, n * nk, chunk * LANE)
    t_op, w_op, v_op, ad, nsteps = _s5_operators(prm, chunk, nk)
    d_t = jnp.tile(d_skip.astype(F32).reshape(q, 1, LANE), (1, 1, chunk))
    width = chunk * LANE
    sw = 2 * gl * p
    yc, sfin = pl.pallas_call(
        functools.partial(_s5_scan_kernel, n=n, nk=nk, nsteps=nsteps),
        grid=(q,),
        in_specs=[pl.BlockSpec((1, n * nk, width), lambda i: (i, 0, 0)),
                  pl.BlockSpec((1, width, width), lambda i: (i, 0, 0)),
                  pl.BlockSpec((1, width, sw), lambda i: (i, 0, 0)),
                  pl.BlockSpec((1, sw, width), lambda i: (i, 0, 0)),
                  pl.BlockSpec((1, ad.shape[1], sw), lambda i: (i, 0, 0)),
                  pl.BlockSpec((1, 1, width), lambda i: (i, 0, 0))],
        out_specs=[pl.BlockSpec((1, n * nk, width), lambda i: (i, 0, 0)),
                   pl.BlockSpec((1, n, sw), lambda i: (i, 0, 0))],
        out_shape=[jax.ShapeDtypeStruct((q, n * nk, width), BF16),
                   jax.ShapeDtypeStruct((q, n, sw), F32)],
        compiler_params=_cp("parallel"),
        name="s5_scan",
    )(uc, t_op, w_op, v_op, ad, d_t)
    y = yc.reshape(q, n, nk, chunk, LANE).transpose(1, 2, 3, 0, 4).reshape(n * t, dm)
    out = _mm_glu(y, w_glu).reshape(n, t, dm)
    sf = sfin.reshape(q, n, 2, gl, p).transpose(2, 1, 0, 3, 4).reshape(2, n, g, p)
    return out, sf[0], sf[1]


def _s5_step_kernel(u_ref, sr_ref, si_ref, bb_ref, cc_ref, a_ref, d_ref, y_ref, or_ref, oi_ref):
    u = u_ref[...]
    bu = jnp.dot(u.astype(BF16), bb_ref[0], preferred_element_type=F32)
    half = bu.shape[1] // 2
    ar, ai = a_ref[0, :, :half], a_ref[0, :, half:]
    sr, si = sr_ref[...], si_ref[...]
    nr = bu[:, :half] + (ar * sr - ai * si)
    ni = bu[:, half:] + (ar * si + ai * sr)
    or_ref[...] = nr
    oi_ref[...] = ni
    s = jnp.concatenate([nr, ni], axis=1).astype(BF16)
    y = jnp.dot(s, cc_ref[0], preferred_element_type=F32) + d_ref[0] * u
    y_ref[...] = jax.nn.gelu(y).astype(BF16)


def _s5_sample(h, s0_re, s0_im, prm, w_in, d_skip, w_glu):
    n, _, dm = h.shape
    g, p = prm[0].shape
    gl = LANE // prm[3].shape[-1]
    q = g // gl
    sw = gl * p
    u = _mm(h.reshape(n, dm), w_in)
    bb, cc, a = _s5_step_operators(prm)
    y, nr, ni = pl.pallas_call(
        _s5_step_kernel,
        grid=(q,),
        in_specs=[pl.BlockSpec((n, LANE), lambda i: (0, i)),
                  pl.BlockSpec((n, sw), lambda i: (0, i)),
                  pl.BlockSpec((n, sw), lambda i: (0, i)),
                  pl.BlockSpec((1, LANE, 2 * sw), lambda i: (i, 0, 0)),
                  pl.BlockSpec((1, 2 * sw, LANE), lambda i: (i, 0, 0)),
                  pl.BlockSpec((1, 1, 2 * sw), lambda i: (i, 0, 0)),
                  pl.BlockSpec((1, 1, LANE), lambda i: (i, 0, 0))],
        out_specs=[pl.BlockSpec((n, LANE), lambda i: (0, i)),
                   pl.BlockSpec((n, sw), lambda i: (0, i)),
                   pl.BlockSpec((n, sw), lambda i: (0, i))],
        out_shape=[jax.ShapeDtypeStruct((n, dm), BF16),
                   jax.ShapeDtypeStruct((n, g * p), F32),
                   jax.ShapeDtypeStruct((n, g * p), F32)],
        compiler_params=_cp("parallel"),
        name="s5_step",
    )(u, s0_re.astype(F32).reshape(n, g * p), s0_im.astype(F32).reshape(n, g * p), bb, cc, a,
      d_skip.astype(F32).reshape(q, 1, LANE))
    out = _mm_glu(y, w_glu).reshape(n, 1, dm)
    return out, nr.reshape(n, g, p), ni.reshape(n, g, p)


def _tri_lower(nrows, ncols, strict):
    r = lax.broadcasted_iota(jnp.int32, (nrows, ncols), 0)
    c = lax.broadcasted_iota(jnp.int32, (nrows, ncols), 1)
    return ((c < r) if strict else (c <= r)).astype(F32)


def _cumsum_kernel(x_ref, o_ref, *, blk):
    t = x_ref.shape[1]
    tri = _tri_lower(blk, blk, strict=False)
    carry = jnp.zeros((1, x_ref.shape[2]), F32)
    for i in range(t // blk):
        xb = x_ref[0, i * blk:(i + 1) * blk, :]
        cb = jnp.dot(tri, xb, preferred_element_type=F32, precision=HI) + carry
        o_ref[0, i * blk:(i + 1) * blk, :] = cb
        carry = cb[blk - 1:blk, :]


def _cumsum_time(x):
    n, t, hh = x.shape
    blk = _pick(t, (128,))
    return pl.pallas_call(
        functools.partial(_cumsum_kernel, blk=blk),
        grid=(n,),
        in_specs=[pl.BlockSpec((1, t, hh), lambda i: (i, 0, 0))],
        out_specs=pl.BlockSpec((1, t, hh), lambda i: (i, 0, 0)),
        out_shape=jax.ShapeDtypeStruct((n, t, hh), F32),
        compiler_params=_cp("parallel"),
        name="cumsum_time",
    )(x)


def _softmax_update(s, m_prev, l_prev):
    m_new = jnp.maximum(m_prev, jnp.max(s, axis=-1, keepdims=True))
    corr = jnp.exp(m_prev - m_new)
    p = jnp.exp(s - m_new)
    return m_new, corr, p, l_prev * corr + jnp.sum(p, axis=-1, keepdims=True)


def _fox_attn_kernel(q_ref, k_ref, v_ref, fq_ref, fk_ref, o_ref, m_ref, l_ref, acc_ref, *, gq, hd, tq, scale):
    qi = pl.program_id(2)
    m_ref[...] = jnp.full(m_ref.shape, -jnp.inf, F32)
    l_ref[...] = jnp.zeros(l_ref.shape, F32)
    acc_ref[...] = jnp.zeros(acc_ref.shape, F32)
    rowpos = qi * tq + lax.broadcasted_iota(jnp.int32, (tq, 1), 0)

    def kv_step(j, carry):
        start = pl.multiple_of(j * tq, tq)
        ks = k_ref[0, pl.ds(start, tq), :]
        vs = v_ref[0, pl.ds(start, tq), :]
        causal = rowpos >= (j * tq + lax.broadcasted_iota(jnp.int32, (1, tq), 1))
        for g in range(gq):
            qg = q_ref[0, :, g * hd:(g + 1) * hd]
            s = lax.dot_general(qg, ks, NT, preferred_element_type=F32) * scale
            s = s + (fq_ref[0, 0, :, g:g + 1] - fk_ref[0, 0, j, g:g + 1, :])
            s = jnp.where(causal, s, -jnp.inf)
            m_new, corr, p, l_new = _softmax_update(s, m_ref[g], l_ref[g])
            acc_ref[g] = acc_ref[g] * corr + jnp.dot(p.astype(BF16), vs, preferred_element_type=F32)
            m_ref[g] = m_new
            l_ref[g] = l_new
        return carry

    lax.fori_loop(0, qi + 1, kv_step, 0)
    for g in range(gq):
        o_ref[0, :, g * hd:(g + 1) * hd] = (acc_ref[g] / l_ref[g]).astype(o_ref.dtype)


def _fox_prompt_attention(q, kvb, f_cum, kvh, hd):
    n, t, qd = q.shape
    gq = qd // (kvh * hd)
    tq = _pick(t, (256, 128))
    nq = t // tq
    fq = f_cum.reshape(n, t, kvh, gq).transpose(0, 2, 1, 3)
    fk = f_cum.reshape(n, nq, tq, kvh, gq).transpose(0, 3, 1, 4, 2)
    return pl.pallas_call(
        functools.partial(_fox_attn_kernel, gq=gq, hd=hd, tq=tq, scale=hd ** -0.5),
        grid=(n, kvh, nq),
        in_specs=[pl.BlockSpec((1, tq, gq * hd), lambda b, k, i: (b, i, k)),
                  pl.BlockSpec((1, t, hd), lambda b, k, i: (b, 0, k)),
                  pl.BlockSpec((1, t, hd), lambda b, k, i: (b, 0, kvh + k)),
                  pl.BlockSpec((1, 1, tq, gq), lambda b, k, i: (b, k, i, 0)),
                  pl.BlockSpec((1, 1, nq, gq, tq), lambda b, k, i: (b, k, 0, 0, 0))],
        out_specs=pl.BlockSpec((1, tq, gq * hd), lambda b, k, i: (b, i, k)),
        out_shape=jax.ShapeDtypeStruct((n, t, qd), BF16),
        scratch_shapes=[pltpu.VMEM((gq, tq, 1), F32), pltpu.VMEM((gq, tq, 1), F32),
                        pltpu.VMEM((gq, tq, hd), F32)],
        compiler_params=_cp("parallel", "parallel", "arbitrary"),
        name="fox_attn",
    )(q, kvb, kvb, fq, fk)


def _mla_attn_kernel(qn_ref, qr_ref, kn_ref, kr_ref, v_ref, o_ref, *, tq, scale):
    qi = pl.program_id(2)
    qn = qn_ref[0]
    qr = qr_ref[0, 0]
    rowpos = qi * tq + lax.broadcasted_iota(jnp.int32, (tq, 1), 0)

    def kv_step(j, carry):
        m_prev, l_prev, acc = carry
        start = pl.multiple_of(j * tq, tq)
        s = lax.dot_general(qn, kn_ref[0, pl.ds(start, tq), :], NT, preferred_element_type=F32)
        s = (s + lax.dot_general(qr, kr_ref[0, pl.ds(start, tq), :], NT, preferred_element_type=F32)) * scale
        causal = rowpos >= (j * tq + lax.broadcasted_iota(jnp.int32, (1, tq), 1))
        s = jnp.where(causal, s, -jnp.inf)
        m_new, corr, p, l_new = _softmax_update(s, m_prev, l_prev)
        acc = acc * corr + jnp.dot(p.astype(BF16), v_ref[0, pl.ds(start, tq), :], preferred_element_type=F32)
        return m_new, l_new, acc

    init = (jnp.full((tq, 1), -jnp.inf, F32), jnp.zeros((tq, 1), F32), jnp.zeros((tq, v_ref.shape[2]), F32))
    _, l, acc = lax.fori_loop(0, qi + 1, kv_step, init)
    o_ref[0] = (acc / l).astype(o_ref.dtype)


def _mla_prompt_attention(qn, qr, kvb, kr, heads, dn, dr, dv):
    n, t, _ = qn.shape
    tq = _pick(t, (256, 128))
    nq = t // tq
    return pl.pallas_call(
        functools.partial(_mla_attn_kernel, tq=tq, scale=(dn + dr) ** -0.5),
        grid=(n, heads, nq),
        in_specs=[pl.BlockSpec((1, tq, dn), lambda b, h, i: (b, i, h)),
                  pl.BlockSpec((1, 1, tq, dr), lambda b, h, i: (b, h, i, 0)),
                  pl.BlockSpec((1, t, dn), lambda b, h, i: (b, 0, 2 * h)),
                  pl.BlockSpec((1, t, dr), lambda b, h, i: (b, 0, 0)),
                  pl.BlockSpec((1, t, dv), lambda b, h, i: (b, 0, 2 * h + 1))],
        out_specs=pl.BlockSpec((1, tq, dv), lambda b, h, i: (b, i, h)),
        out_shape=jax.ShapeDtypeStruct((n, t, heads * dv), BF16),
        compiler_params=_cp("parallel", "parallel", "arbitrary"),
        name="mla_attn",
    )(qn, qr, kvb, kr, kvb)


def _row_to_col(row):
    hh = row.shape[1]
    eye = (lax.broadcasted_iota(jnp.int32, (hh, hh), 0) == lax.broadcasted_iota(jnp.int32, (hh, hh), 1))
    return jnp.sum(jnp.where(eye, row, 0.0), axis=1, keepdims=True)


def _to_head_major(p_bf16):
    hh = p_bf16.shape[1]
    eye = (lax.broadcasted_iota(jnp.int32, (hh, hh), 0)
           == lax.broadcasted_iota(jnp.int32, (hh, hh), 1)).astype(BF16)
    return lax.dot_general(eye, p_bf16, NT, preferred_element_type=F32).astype(BF16)


def _fox_decode_kernel(pt_ref, *refs, pp, kvh, gq, hd, scale):
    del pt_ref
    q_ref, knew_ref, vnew_ref, cnew_ref = refs[:4]
    k_refs = refs[4:4 + pp]
    v_refs = refs[4 + pp:4 + 2 * pp]
    f_refs = refs[4 + 2 * pp:4 + 3 * pp]
    o_ref = refs[4 + 3 * pp]
    m_ref, l_ref, acc_ref, carry_ref = refs[5 + 3 * pp:]
    step = pl.program_id(1)
    hh = kvh * gq
    ps = k_refs[0].shape[1]

    @pl.when(step == 0)
    def _():
        m_ref[...] = jnp.full(m_ref.shape, -jnp.inf, F32)
        l_ref[...] = jnp.zeros(l_ref.shape, F32)
        acc_ref[...] = jnp.zeros(acc_ref.shape, F32)
        carry_ref[...] = jnp.zeros(carry_ref.shape, F32)

    qbd = q_ref[0]
    cnew = cnew_ref[0]
    r_i = lax.broadcasted_iota(jnp.int32, (ps, ps), 0)
    c_i = lax.broadcasted_iota(jnp.int32, (ps, ps), 1)
    upper = (c_i > r_i).astype(F32)
    carry = carry_ref[...]
    scores = []
    for i in range(pp):
        kb = k_refs[i][0].astype(BF16)
        s = lax.dot_general(kb, qbd, NT, preferred_element_type=F32) * scale
        lf = f_refs[i][0]
        after = jnp.dot(upper, lf, preferred_element_type=F32, precision=HI) + carry
        scores.append(s + (cnew + after))
        carry = carry + jnp.sum(lf, axis=0, keepdims=True)
    carry_ref[...] = carry
    s_all = jnp.concatenate(scores, axis=0)
    m_prev = m_ref[...]
    m_new = jnp.maximum(m_prev, jnp.max(s_all, axis=0, keepdims=True))
    corr = jnp.exp(m_prev - m_new)
    p_all = jnp.exp(s_all - m_new)
    l_ref[...] = l_ref[...] * corr + jnp.sum(p_all, axis=0, keepdims=True)
    m_ref[...] = m_new
    acc = acc_ref[...] * _row_to_col(corr)
    for i in range(pp):
        ph = _to_head_major(p_all[i * ps:(i + 1) * ps].astype(BF16))
        acc = acc + jnp.dot(ph, v_refs[i][0].astype(BF16), preferred_element_type=F32)
    acc_ref[...] = acc

    @pl.when(step == pl.num_programs(1) - 1)
    def _():
        kn = knew_ref[0].astype(BF16)
        s_new = lax.dot_general(kn, qbd, NT, preferred_element_type=F32) * scale + (cnew - cnew)
        m_prev = m_ref[...]
        m_new = jnp.maximum(m_prev, s_new)
        corr = jnp.exp(m_prev - m_new)
        p_new = jnp.exp(s_new - m_new)
        l_fin = l_ref[...] * corr + p_new
        p_col = _row_to_col(p_new.astype(BF16).astype(F32))
        vn = vnew_ref[0].astype(BF16).astype(F32)
        acc = acc_ref[...] * _row_to_col(corr) + p_col * vn
        out = acc / _row_to_col(l_fin)
        for kk in range(kvh):
            o_ref[0, kk * gq:(kk + 1) * gq, :] = out[kk * gq:(kk + 1) * gq, kk * hd:(kk + 1) * hd]


def _fox_sample_attention(q, k_new, v_new, logf_new, cache_k, cache_v, cache_logf, layer, page_table):
    n = q.shape[0]
    _, pool, ps, kvh, hd = cache_k.shape
    hh = cache_logf.shape[-1]
    gq = hh // kvh
    npg = page_table.shape[1]
    pp = _pick(npg, (16, 8, 4, 2, 1))
    kd = kvh * hd
    ck = cache_k.reshape(-1, ps, kd)
    cv = cache_v.reshape(-1, ps, kd)
    cf = cache_logf.reshape(-1, ps, hh)
    own = (jnp.arange(hh)[:, None] // gq == jnp.arange(kvh)[None, :])
    qbd = jnp.where(own[None, :, :, None], q.reshape(n, hh, 1, hd), 0.0).reshape(n, hh, kd).astype(BF16)
    base = layer * pool

    def page_map(i):
        return lambda s, p, pt: (base + pt[s, npg - 1 - (p * pp + i)], 0, 0)

    seq3 = lambda s, p, pt: (s, 0, 0)
    in_specs = [pl.BlockSpec((1, hh, kd), seq3),
                pl.BlockSpec((1, 1, kd), seq3),
                pl.BlockSpec((1, 1, kd), seq3),
                pl.BlockSpec((1, 1, hh), seq3)]
    in_specs += [pl.BlockSpec((1, ps, kd), page_map(i)) for i in range(pp)]
    in_specs += [pl.BlockSpec((1, ps, kd), page_map(i)) for i in range(pp)]
    in_specs += [pl.BlockSpec((1, ps, hh), page_map(i)) for i in range(pp)]
    out = pl.pallas_call(
        functools.partial(_fox_decode_kernel, pp=pp, kvh=kvh, gq=gq, hd=hd, scale=hd ** -0.5),
        grid_spec=pltpu.PrefetchScalarGridSpec(
            num_scalar_prefetch=1,
            grid=(n, npg // pp),
            in_specs=in_specs,
            out_specs=pl.BlockSpec((1, hh, hd), seq3),
            scratch_shapes=[pltpu.VMEM((1, hh), F32), pltpu.VMEM((1, hh), F32),
                            pltpu.VMEM((hh, kd), F32), pltpu.VMEM((1, hh), F32)]),
        out_shape=jax.ShapeDtypeStruct((n, hh, hd), F32),
        compiler_params=_cp("parallel", "arbitrary"),
        name="fox_decode",
    )(page_table, qbd, k_new.reshape(n, 1, kd), v_new.reshape(n, 1, kd), logf_new.reshape(n, 1, hh),
      *([ck] * pp), *([cv] * pp), *([cf] * pp))
    return out.reshape(n, hh * hd).astype(BF16)


def _mla_decode_kernel(pt_ref, *refs, pp, scale):
    del pt_ref
    ql_ref, qr_ref, cnew_ref, rnew_ref = refs[:4]
    c_refs = refs[4:4 + pp]
    r_refs = refs[4 + pp:4 + 2 * pp]
    o_ref = refs[4 + 2 * pp]
    m_ref, l_ref, acc_ref = refs[5 + 2 * pp:]
    step = pl.program_id(1)
    ps = c_refs[0].shape[1]

    @pl.when(step == 0)
    def _():
        m_ref[...] = jnp.full(m_ref.shape, -jnp.inf, F32)
        l_ref[...] = jnp.zeros(l_ref.shape, F32)
        acc_ref[...] = jnp.zeros(acc_ref.shape, F32)

    ql = ql_ref[0]
    qr = qr_ref[0]
    cbs, scores = [], []
    for i in range(pp):
        cb = c_refs[i][0].astype(BF16)
        rb = r_refs[i][0].astype(BF16)
        s = lax.dot_general(cb, ql, NT, preferred_element_type=F32)
        s = (s + lax.dot_general(rb, qr, NT, preferred_element_type=F32)) * scale
        cbs.append(cb)
        scores.append(s)
    s_all = jnp.concatenate(scores, axis=0)
    m_prev = m_ref[...]
    m_new = jnp.maximum(m_prev, jnp.max(s_all, axis=0, keepdims=True))
    corr = jnp.exp(m_prev - m_new)
    p_all = jnp.exp(s_all - m_new)
    l_ref[...] = l_ref[...] * corr + jnp.sum(p_all, axis=0, keepdims=True)
    m_ref[...] = m_new
    acc = acc_ref[...] * _row_to_col(corr)
    for i in range(pp):
        ph = _to_head_major(p_all[i * ps:(i + 1) * ps].astype(BF16))
        acc = acc + jnp.dot(ph, cbs[i], preferred_element_type=F32)
    acc_ref[...] = acc

    @pl.when(step == pl.num_programs(1) - 1)
    def _():
        cn = cnew_ref[0].astype(BF16)
        rn = rnew_ref[0].astype(BF16)
        s_new = (lax.dot_general(cn, ql, NT, preferred_element_type=F32)
                 + lax.dot_general(rn, qr, NT, preferred_element_type=F32)) * scale
        m_prev = m_ref[...]
        m_new = jnp.maximum(m_prev, s_new)
        corr = jnp.exp(m_prev - m_new)
        p_new = jnp.exp(s_new - m_new)
        l_fin = l_ref[...] * corr + p_new
        p_col = _row_to_col(p_new.astype(BF16).astype(F32))
        acc = acc_ref[...] * _row_to_col(corr) + p_col * cn.astype(F32)
        o_ref[0] = (acc / _row_to_col(l_fin)).astype(o_ref.dtype)


def _mla_sample_attention(q_lat, q_rot, ckv_new, kpe_new, cache_ckv, cache_kpe, layer, page_table, heads, scale):
    n = q_lat.shape[0]
    _, pool, ps, kl = cache_ckv.shape
    rr = cache_kpe.shape[-1]
    npg = page_table.shape[1]
    pp = _pick(npg, (16, 8, 4, 2, 1))
    cc = cache_ckv.reshape(-1, ps, kl)
    cr = cache_kpe.reshape(-1, ps, rr)
    base = layer * pool

    def page_map(i):
        return lambda s, p, pt: (base + pt[s, p * pp + i], 0, 0)

    seq3 = lambda s, p, pt: (s, 0, 0)
    in_specs = [pl.BlockSpec((1, heads, kl), seq3), pl.BlockSpec((1, heads, rr), seq3),
                pl.BlockSpec((1, 1, kl), seq3), pl.BlockSpec((1, 1, rr), seq3)]
    in_specs += [pl.BlockSpec((1, ps, kl), page_map(i)) for i in range(pp)]
    in_specs += [pl.BlockSpec((1, ps, rr), page_map(i)) for i in range(pp)]
    out = pl.pallas_call(
        functools.partial(_mla_decode_kernel, pp=pp, scale=scale),
        grid_spec=pltpu.PrefetchScalarGridSpec(
            num_scalar_prefetch=1,
            grid=(n, npg // pp),
            in_specs=in_specs,
            out_specs=pl.BlockSpec((1, heads, kl), seq3),
            scratch_shapes=[pltpu.VMEM((1, heads), F32), pltpu.VMEM((1, heads), F32),
                            pltpu.VMEM((heads, kl), F32)]),
        out_shape=jax.ShapeDtypeStruct((n, heads, kl), BF16),
        compiler_params=_cp("parallel", "arbitrary"),
        name="mla_decode",
    )(page_table, q_lat.reshape(n, heads, kl), q_rot.reshape(n, heads, rr),
      ckv_new.reshape(n, 1, kl), kpe_new.reshape(n, 1, rr), *([cc] * pp), *([cr] * pp))
    return out.reshape(n, heads * kl)


def _mla_norm_kernel(z_ref, gq_ref, gkv_ref, cq_ref, ckv_ref, ckvb_ref, *, ql):
    z = z_ref[...]
    zq, zk = z[:, :ql], z[:, ql:]
    cq = zq * lax.rsqrt(jnp.mean(zq * zq, axis=-1, keepdims=True) + RMS_EPS) * gq_ref[...]
    ck = zk * lax.rsqrt(jnp.mean(zk * zk, axis=-1, keepdims=True) + RMS_EPS) * gkv_ref[...]
    cq_ref[...] = cq.astype(BF16)
    ckv_ref[...] = ck
    ckvb_ref[...] = ck.astype(BF16)


def _mla_norm(z, g_q, g_kv):
    m, w = z.shape
    ql = g_q.shape[0]
    kl = g_kv.shape[0]
    tm = _pick(m, (512, 256, 128))
    return pl.pallas_call(
        functools.partial(_mla_norm_kernel, ql=ql),
        grid=(m // tm,),
        in_specs=[pl.BlockSpec((tm, w), lambda i: (i, 0)),
                  pl.BlockSpec((1, ql), lambda i: (0, 0)),
                  pl.BlockSpec((1, kl), lambda i: (0, 0))],
        out_specs=[pl.BlockSpec((tm, ql), lambda i: (i, 0)),
                   pl.BlockSpec((tm, kl), lambda i: (i, 0)),
                   pl.BlockSpec((tm, kl), lambda i: (i, 0))],
        out_shape=[jax.ShapeDtypeStruct((m, ql), BF16), jax.ShapeDtypeStruct((m, kl), F32),
                   jax.ShapeDtypeStruct((m, kl), BF16)],
        compiler_params=_cp("parallel"),
        name="mla_norm",
    )(z, g_q.astype(F32).reshape(1, ql), g_kv.astype(F32).reshape(1, kl))


def _rope_tables(pos, half, reps):
    inv = ROPE_BASE ** (-jnp.arange(half, dtype=F32) / half)
    ang = pos[:, None] * inv
    cos, sin = jnp.cos(ang), jnp.sin(ang)
    return (jnp.tile(jnp.concatenate([cos, cos], axis=-1), (1, reps)),
            jnp.tile(jnp.concatenate([-sin, sin], axis=-1), (1, reps)))


def _swap_halves(w, width):
    k, n = w.shape
    return w.reshape(k, n // width, 2, width // 2)[:, :, ::-1, :].reshape(k, n)


def _topk_values(x, k):
    nrow = x.shape[0]
    idx = lax.broadcasted_iota(jnp.int32, x.shape, 0).astype(F32)
    outs = []
    for _ in range(k):
        mx = jnp.max(x, axis=0, keepdims=True)
        outs.append(mx)
        first = jnp.min(jnp.where(x == mx, idx, float(nrow)), axis=0, keepdims=True)
        x = jnp.where(idx == first, -jnp.inf, x)
    return jnp.concatenate(outs, axis=0)


def _peer_route_kernel(qp_ref, sk_ref, s1_ref, c1_ref, s2_ref, e2_ref, tau_ref, *, heads, nkeys, topk):
    for h in range(heads):
        sts, vals = [], []
        for c in range(2):
            qh = qp_ref[:, (2 * h + c) * nkeys:(2 * h + c + 1) * nkeys]
            st = lax.dot_general(sk_ref[c, h], qh, NT, preferred_element_type=F32, precision=HI)
            sts.append(st)
            vals.append(_topk_values(st, topk))
        v1, v2 = vals
        cand = jnp.concatenate([v1[r:r + 1, :] + v2 for r in range(topk)], axis=0)
        top = _topk_values(cand, topk)
        z = jnp.sum(jnp.exp(top - top[0:1]), axis=0, keepdims=True)
        s1_ref[h] = sts[0]
        s2_ref[h] = sts[1]
        c1_ref[h] = jnp.exp(sts[0] - v1[0:1]) / z
        e2_ref[h] = jnp.exp(sts[1] - v2[0:1])
        tau_ref[h:h + 1, :] = top[topk - 1:topk]


def _peer_dense_kernel(h_ref, u_ref, v_ref, s1_ref, c1_ref, s2_ref, e2_ref, tau_ref, o_ref, ht_ref,
                       *, heads, nkeys):
    e = pl.program_id(1)
    eb = u_ref.shape[0]
    tt = h_ref.shape[0]

    @pl.when(e == 0)
    def _():
        ht_ref[...] = h_ref[...].astype(F32).T.astype(BF16)
        o_ref[...] = jnp.zeros(o_ref.shape, F32)

    st = jnp.dot(u_ref[...], ht_ref[...], preferred_element_type=F32)
    act = jax.nn.gelu(st)
    parts = []
    for al in range(eb // nkeys):
        a = e * (eb // nkeys) + al
        w = jnp.zeros((nkeys, tt), F32)
        for h in range(heads):
            sc = s2_ref[h] + s1_ref[h, pl.ds(a, 1), :]
            w = w + jnp.where(sc >= tau_ref[h:h + 1, :], e2_ref[h] * c1_ref[h, pl.ds(a, 1), :], 0.0)
        parts.append(w * act[al * nkeys:(al + 1) * nkeys])
    pt = parts[0] if len(parts) == 1 else jnp.concatenate(parts, axis=0)
    o_ref[...] += jnp.dot(pt.T.astype(BF16), v_ref[...], preferred_element_type=F32)


def _peer_ffn(h, w_q, subkeys, u_tab, v_tab):
    m, d = h.shape
    _, heads, nkeys, kd2 = subkeys.shape
    assert nkeys == LANE and kd2 == LANE, "product-key halves are laid out on one 128-lane tile"
    ne = u_tab.shape[0]
    qp = _mm(h, w_q)
    tt = _pick(m, (256, 128))
    tab = jax.ShapeDtypeStruct((heads, nkeys, m), F32)
    tspec = pl.BlockSpec((heads, nkeys, tt), lambda i: (0, 0, i))
    s1, c1, s2, e2, tau = pl.pallas_call(
        functools.partial(_peer_route_kernel, heads=heads, nkeys=nkeys, topk=PEER_TOPK),
        grid=(m // tt,),
        in_specs=[pl.BlockSpec((tt, qp.shape[1]), lambda i: (i, 0)),
                  pl.BlockSpec((2, heads, nkeys, kd2), lambda i: (0, 0, 0, 0))],
        out_specs=[tspec, tspec, tspec, tspec, pl.BlockSpec((heads, tt), lambda i: (0, i))],
        out_shape=[tab, tab, tab, tab, jax.ShapeDtypeStruct((heads, m), F32)],
        compiler_params=_cp("parallel"),
        name="peer_route",
    )(qp, subkeys.astype(F32))
    td = _pick(m, (512, 256, 128))
    eb = _pick(ne, (512, 256, 128))
    once = pl.Buffered(1)
    dspec = pl.BlockSpec((heads, nkeys, td), lambda i, e: (0, 0, i), pipeline_mode=once)
    return pl.pallas_call(
        functools.partial(_peer_dense_kernel, heads=heads, nkeys=nkeys),
        grid=(m // td, ne // eb),
        in_specs=[pl.BlockSpec((td, d), lambda i, e: (i, 0), pipeline_mode=once),
                  pl.BlockSpec((eb, d), lambda i, e: (e, 0)),
                  pl.BlockSpec((eb, d), lambda i, e: (e, 0)),
                  dspec, dspec, dspec, dspec,
                  pl.BlockSpec((heads, td), lambda i, e: (0, i), pipeline_mode=once)],
        out_specs=pl.BlockSpec((td, d), lambda i, e: (i, 0)),
        out_shape=jax.ShapeDtypeStruct((m, d), F32),
        scratch_shapes=[pltpu.VMEM((d, td), BF16)],
        compiler_params=_cp("parallel", "arbitrary"),
        name="peer_dense",
    )(h, u_tab, v_tab, s1, c1, s2, e2, tau)


def kernel(x_prompt, x_sample, state_s5_re, state_s5_im, cache_fox_k, cache_fox_v, cache_fox_logf, cache_mla_ckv, cache_mla_kpe, page_table, c_prompt, c_sample, ada_w, ada_b, ln_g, ln_b, s5_w_in, s5_lam_re, s5_lam_im, s5_log_step, s5_b_re, s5_b_im, s5_c_re, s5_c_im, s5_d, s5_w_glu, fox_w_in, fox_b_f, fox_w_o, mla_w_in, mla_q_norm, mla_kv_norm, mla_w_uq, mla_w_ukv, mla_w_o, peer_w_q, peer_subkeys, peer_u, peer_v):
    n_p, t_p, dm = x_prompt.shape
    n_s, t_s, _ = x_sample.shape
    assert t_s == 1, "the sample group decodes one token per sequence"
    depth = ada_w.shape[0]
    alpha = (2 * depth) ** 0.25
    past_len = page_table.shape[1] * cache_fox_k.shape[2]
    fox_heads = fox_b_f.shape[-1]
    fox_kvh, fox_hd = cache_fox_k.shape[3], cache_fox_k.shape[4]
    mla_ql, mla_kl, mla_r = mla_q_norm.shape[-1], mla_kv_norm.shape[-1], cache_mla_kpe.shape[-1]
    mla_heads = (mla_w_uq.shape[-1] - mla_w_ukv.shape[-1] + mla_w_o.shape[-2]) // mla_r
    mla_dv = mla_w_o.shape[-2] // mla_heads
    mla_dn = mla_w_ukv.shape[-1] // mla_heads - mla_dv
    assert mla_dn == mla_dv, "k_nope and v column blocks of w_ukv are addressed with one block width"

    rows = n_p + n_s
    rpad = -(-rows // 8) * 8
    c_all = jnp.pad(jnp.concatenate([c_prompt, c_sample], axis=0).astype(F32), ((0, rpad - rows), (0, 0)))
    mod = _modulation_all(c_all, ada_w.astype(F32), ada_b.astype(F32))

    def mods(i, s):
        m = mod[2 * i + s]
        mp = m[:n_p, None, :]
        ms = m[n_p:rows][None]
        cut = lambda a: (a[..., :dm], a[..., dm:2 * dm], a[..., 2 * dm:])
        return cut(mp), cut(ms)

    pos_p = jnp.arange(t_p, dtype=F32)
    pos_s = past_len + jnp.arange(t_s, dtype=F32)
    cos_p, sin_p = _rope_tables(pos_p, mla_r // 2, LANE // mla_r)
    cos_p, sin_p = jnp.tile(cos_p, (n_p, 1)), jnp.tile(sin_p, (n_p, 1))
    cos_s, sin_s = _rope_tables(jnp.tile(pos_s, (n_s,)), mla_r // 2, LANE // mla_r)

    yp, ys = x_prompt.astype(F32), x_sample.astype(F32).reshape(1, n_s, dm)
    (shp, scp, _), (shs, scs, _) = mods(0, 0)
    hp = _ln_mod(yp, scale=scp, shift=shp, alpha=alpha)
    hs = _ln_mod(ys, scale=scs, shift=shs, alpha=alpha)

    out = {k: [] for k in ("s5r_p", "s5i_p", "s5r_s", "s5i_s", "fk_p", "fv_p", "fl_p", "fk_s", "fv_s", "fl_s",
                           "ckv_p", "kpe_p", "ckv_s", "kpe_s")}
    for i in range(depth):
        kind, j = i % 3, i // 3
        (_, _, gate_p), (_, _, gate_s) = mods(i, 0)
        if kind == 0:
            prm = (s5_lam_re[j], s5_lam_im[j], s5_log_step[j], s5_b_re[j], s5_b_im[j], s5_c_re[j], s5_c_im[j])
            w_in = s5_w_in[j].astype(BF16)
            w_glu = s5_w_glu[j].astype(BF16)
            op, sr, si = _s5_prompt(hp, prm, w_in, s5_d[j], w_glu)
            out["s5r_p"].append(sr)
            out["s5i_p"].append(si)
            os_, sr, si = _s5_sample(hs.reshape(n_s, 1, dm), state_s5_re[j], state_s5_im[j], prm, w_in,
                                     s5_d[j], w_glu)
            out["s5r_s"].append(sr)
            out["s5i_s"].append(si)
            os_ = os_.reshape(1, n_s, dm)
        elif kind == 1:
            nq, nkv = fox_heads * fox_hd, fox_kvh * fox_hd
            w = fox_w_in[j].astype(BF16)
            w_q, w_kv, w_f = w[:, :nq], w[:, nq:nq + 2 * nkv], w[:, nq + 2 * nkv:]
            w_o = fox_w_o[j].astype(BF16)
            h2 = hp.reshape(n_p * t_p, dm)
            q = _mm(h2, w_q, (BF16,))
            kv, kvb = _mm(h2, w_kv, (F32, BF16))
            lf = _mm_logsig(h2, w_f, fox_b_f[j])
            f_cum = _cumsum_time(lf.reshape(n_p, t_p, fox_heads))
            o = _fox_prompt_attention(q.reshape(n_p, t_p, nq), kvb.reshape(n_p, t_p, 2 * nkv), f_cum,
                                      fox_kvh, fox_hd)
            op = _mm(o.reshape(n_p * t_p, nq), w_o).reshape(n_p, t_p, dm)
            out["fk_p"].append(kv[:, :nkv].reshape(n_p, t_p, fox_kvh, fox_hd))
            out["fv_p"].append(kv[:, nkv:].reshape(n_p, t_p, fox_kvh, fox_hd))
            out["fl_p"].append(lf.reshape(n_p, t_p, fox_heads))
            h2 = hs.reshape(n_s, dm)
            q = _mm(h2, w_q)
            kv = _mm(h2, w_kv)
            lf = _mm_logsig(h2, w_f, fox_b_f[j])
            o = _fox_sample_attention(q, kv[:, :nkv], kv[:, nkv:], lf, cache_fox_k, cache_fox_v,
                                      cache_fox_logf, j, page_table)
            os_ = _mm(o, w_o).reshape(1, n_s, dm)
            out["fk_s"].append(kv[:, :nkv].reshape(n_s, 1, fox_kvh, fox_hd))
            out["fv_s"].append(kv[:, nkv:].reshape(n_s, 1, fox_kvh, fox_hd))
            out["fl_s"].append(lf.reshape(n_s, 1, fox_heads))
        else:
            w = mla_w_in[j].astype(BF16)
            w_lat, w_pe = w[:, :mla_ql + mla_kl], w[:, mla_ql + mla_kl:]
            w_pe_rot = _swap_halves(w_pe, mla_r)
            wq = mla_w_uq[j].astype(BF16).reshape(mla_ql, mla_heads, mla_dn + mla_r)
            wq_n = wq[:, :, :mla_dn].reshape(mla_ql, mla_heads * mla_dn)
            wq_r = wq[:, :, mla_dn:].reshape(mla_ql, mla_heads * mla_r)
            wq_r_rot = _swap_halves(wq_r, mla_r)
            w_ukv = mla_w_ukv[j].astype(BF16)
            w_o = mla_w_o[j].astype(BF16)

            def project(h2, cos, sin):
                z = _mm(h2, w_lat)
                cq, ckv, ckvb = _mla_norm(z, mla_q_norm[j], mla_kv_norm[j])
                kpe = _mm_rope(h2, w_pe, w_pe_rot, cos[:, :mla_r], sin[:, :mla_r], (F32,))
                qn = _mm(cq, wq_n, (BF16,))
                qr = _mm_rope(cq, wq_r, wq_r_rot, cos, sin, (BF16,))
                return qn, qr, ckv, ckvb, kpe

            qn, qr, ckv, ckvb, kpe = project(hp.reshape(n_p * t_p, dm), cos_p, sin_p)
            kvb = _mm(ckvb, w_ukv, (BF16,))
            qr4 = qr.reshape(n_p, t_p, mla_heads, mla_r).transpose(0, 2, 1, 3)
            o = _mla_prompt_attention(qn.reshape(n_p, t_p, -1), qr4, kvb.reshape(n_p, t_p, -1),
                                      kpe.astype(BF16).reshape(n_p, t_p, mla_r), mla_heads, mla_dn, mla_r, mla_dv)
            op = _mm(o.reshape(n_p * t_p, -1), w_o).reshape(n_p, t_p, dm)
            out["ckv_p"].append(ckv.reshape(n_p, t_p, mla_kl))
            out["kpe_p"].append(kpe.reshape(n_p, t_p, mla_r))
            qn, qr, ckv, ckvb, kpe = project(hs.reshape(n_s, dm), cos_s, sin_s)
            q_lat = _head_mm(qn, w_ukv, mla_heads, 0, True, BF16)
            a = _mla_sample_attention(q_lat, qr, ckv, kpe, cache_mla_ckv, cache_mla_kpe, j, page_table, mla_heads,
                                      (mla_dn + mla_r) ** -0.5)
            o = _head_mm(a, w_ukv, mla_heads, 1, False, BF16)
            os_ = _mm(o, w_o).reshape(1, n_s, dm)
            out["ckv_s"].append(ckv.reshape(n_s, 1, mla_kl))
            out["kpe_s"].append(kpe.reshape(n_s, 1, mla_r))
        (shp, scp, gate2_p), (shs, scs, gate2_s) = mods(i, 1)
        yp, hp = _ln_mod(yp, op, gate_p, ln_g[i, 0], ln_b[i, 0], scp, shp, alpha=alpha)
        ys, hs = _ln_mod(ys, os_, gate_s, ln_g[i, 0], ln_b[i, 0], scs, shs, alpha=alpha)
        w_q = peer_w_q[i].astype(BF16)
        u_tab, v_tab = peer_u[i].astype(BF16), peer_v[i].astype(BF16)
        fp = _peer_ffn(hp.reshape(n_p * t_p, dm), w_q, peer_subkeys[i], u_tab, v_tab).reshape(n_p, t_p, dm)
        fs = _peer_ffn(hs.reshape(n_s, dm), w_q, peer_subkeys[i], u_tab, v_tab).reshape(1, n_s, dm)
        if i + 1 < depth:
            (shp, scp, _), (shs, scs, _) = mods(i + 1, 0)
            yp, hp = _ln_mod(yp, fp, gate2_p, ln_g[i, 1], ln_b[i, 1], scp, shp, alpha=alpha)
            ys, hs = _ln_mod(ys, fs, gate2_s, ln_g[i, 1], ln_b[i, 1], scs, shs, alpha=alpha)
        else:
            yp = _ln_mod(yp, fp, gate2_p, ln_g[i, 1], ln_b[i, 1], alpha=alpha)
            ys = _ln_mod(ys, fs, gate2_s, ln_g[i, 1], ln_b[i, 1], alpha=alpha)

    st = lambda k: jnp.stack(out[k])
    return (yp, ys.reshape(n_s, 1, dm),
            st("s5r_p"), st("s5i_p"), st("fk_p"), st("fv_p"), st("fl_p"), st("ckv_p"), st("kpe_p"),
            st("s5r_s"), st("s5i_s"), st("fk_s"), st("fv_s"), st("fl_s"), st("ckv_s"), st("kpe_s"))
```

```python
import functools
import math

import jax
import jax.numpy as jnp
from jax import lax
from jax.experimental import pallas as pl
from jax.experimental.pallas import tpu as pltpu

F32 = jnp.float32
BF16 = jnp.bfloat16
HI = lax.Precision.HIGHEST

LANE = 128
SUBLANE = 8
VMEM_LIMIT = 56 * 1024 * 1024
LN_EPS = 1e-5
RMS_EPS = 1e-6
ROPE_BASE = 10000.0
S5_CHUNK = 8
PEER_TOPK = 16
NT = (((1,), (1,)), ((), ()))


def _cp(*sem):
    return pltpu.CompilerParams(dimension_semantics=sem, vmem_limit_bytes=VMEM_LIMIT)


def _pick(n, cands):
    for c in cands:
        if c <= n and n % c == 0:
            return c
    return n


def _wspec(w, k, tn, layer, col0, ngrid):
    off = col0 // tn
    assert col0 % tn == 0
    if w.ndim == 3:
        if ngrid == 1:
            return pl.BlockSpec((None, k, tn), lambda j: (layer, 0, off + j))
        return pl.BlockSpec((None, k, tn), lambda i, j: (layer, 0, off + j))
    if ngrid == 1:
        return pl.BlockSpec((k, tn), lambda j: (0, off + j))
    return pl.BlockSpec((k, tn), lambda i, j: (0, off + j))


def _mm_kernel(x_ref, w_ref, *o_refs):
    acc = jnp.dot(x_ref[...], w_ref[...], preferred_element_type=F32)
    for o_ref in o_refs:
        o_ref[...] = acc.astype(o_ref.dtype)


def _mm(x, w, out_dtypes=(F32,), layer=0, col0=0, ncols=None):
    m, k = x.shape
    n = w.shape[-1] - col0 if ncols is None else ncols
    tm = _pick(m, (1024, 512, 256, 128))
    tn = _pick(math.gcd(n, col0) if col0 else n, (1024, 512, 256, 128))
    outs = pl.pallas_call(
        _mm_kernel,
        grid=(m // tm, n // tn),
        in_specs=[pl.BlockSpec((tm, k), lambda i, j: (i, 0)), _wspec(w, k, tn, layer, col0, 2)],
        out_specs=[pl.BlockSpec((tm, tn), lambda i, j: (i, j)) for _ in out_dtypes],
        out_shape=[jax.ShapeDtypeStruct((m, n), d) for d in out_dtypes],
        compiler_params=_cp("parallel", "arbitrary"),
        name="mm",
    )(x, w)
    return outs[0] if len(out_dtypes) == 1 else outs


def _glu_kernel(x_ref, wv_ref, wg_ref, o_ref):
    x = x_ref[...]
    val = jnp.dot(x, wv_ref[...], preferred_element_type=F32)
    gate = jnp.dot(x, wg_ref[...], preferred_element_type=F32)
    o_ref[...] = val * jax.nn.sigmoid(gate)


def _mm_glu(x, w, layer):
    m, k = x.shape
    n = w.shape[-1] // 2
    tm = _pick(m, (1024, 512, 256, 128))
    tn = _pick(n, (512, 256, 128))
    return pl.pallas_call(
        _glu_kernel,
        grid=(m // tm, n // tn),
        in_specs=[pl.BlockSpec((tm, k), lambda i, j: (i, 0)),
                  _wspec(w, k, tn, layer, 0, 2), _wspec(w, k, tn, layer, n, 2)],
        out_specs=pl.BlockSpec((tm, tn), lambda i, j: (i, j)),
        out_shape=jax.ShapeDtypeStruct((m, n), F32),
        compiler_params=_cp("parallel", "arbitrary"),
        name="mm_glu",
    )(x, w, w)


def _logsig_kernel(x_ref, w_ref, b_ref, o_ref):
    z = jnp.dot(x_ref[...], w_ref[...], preferred_element_type=F32) + b_ref[...]
    o_ref[...] = jnp.minimum(z, 0.0) - jnp.log1p(jnp.exp(-jnp.abs(z)))


def _mm_logsig(x, w, b):
    m, k = x.shape
    n = w.shape[1]
    tm = _pick(m, (1024, 512, 256, 128))
    return pl.pallas_call(
        _logsig_kernel,
        grid=(m // tm,),
        in_specs=[pl.BlockSpec((tm, k), lambda i: (i, 0)),
                  pl.BlockSpec((k, n), lambda i: (0, 0)),
                  pl.BlockSpec((1, n), lambda i: (0, 0))],
        out_specs=pl.BlockSpec((tm, n), lambda i: (i, 0)),
        out_shape=jax.ShapeDtypeStruct((m, n), F32),
        compiler_params=_cp("parallel"),
        name="mm_logsig",
    )(x, w, b.reshape(1, n).astype(F32))


def _rope_kernel(x_ref, w_ref, wr_ref, cos_ref, sin_ref, *o_refs):
    x = x_ref[...]
    a = jnp.dot(x, w_ref[...], preferred_element_type=F32)
    b = jnp.dot(x, wr_ref[...], preferred_element_type=F32)
    reps = a.shape[1] // cos_ref.shape[1]
    cos, sin = cos_ref[...], sin_ref[...]
    if reps > 1:
        cos = jnp.concatenate([cos] * reps, axis=1)
        sin = jnp.concatenate([sin] * reps, axis=1)
    out = a * cos + b * sin
    for o_ref in o_refs:
        o_ref[...] = out.astype(o_ref.dtype)


def _mm_rope(x, w, w_rot, cos, sin, out_dtypes):
    m, k = x.shape
    n = w.shape[1]
    period = cos.shape[1]
    tm = _pick(m, (1024, 512, 256, 128))
    tn = _pick(n, (1024, 512, 256, 128)) if n % period == 0 and period % LANE == 0 else n
    outs = pl.pallas_call(
        _rope_kernel,
        grid=(m // tm, n // tn),
        in_specs=[pl.BlockSpec((tm, k), lambda i, j: (i, 0)),
                  pl.BlockSpec((k, tn), lambda i, j: (0, j)),
                  pl.BlockSpec((k, tn), lambda i, j: (0, j)),
                  pl.BlockSpec((tm, period), lambda i, j: (i, 0)),
                  pl.BlockSpec((tm, period), lambda i, j: (i, 0))],
        out_specs=[pl.BlockSpec((tm, tn), lambda i, j: (i, j)) for _ in out_dtypes],
        out_shape=[jax.ShapeDtypeStruct((m, n), d) for d in out_dtypes],
        compiler_params=_cp("parallel", "arbitrary"),
        name="mm_rope",
    )(x, w, w_rot, cos, sin)
    return outs[0] if len(out_dtypes) == 1 else outs


def _headmm_kernel(x_ref, w_ref, o_ref, *, trans_w):
    if trans_w:
        acc = lax.dot_general(x_ref[...], w_ref[...], NT, preferred_element_type=F32)
    else:
        acc = jnp.dot(x_ref[...], w_ref[...], preferred_element_type=F32)
    o_ref[...] = acc.astype(o_ref.dtype)


def _head_mm(x, w, layer, heads, w_off, trans_w, out_dtype):
    m = x.shape[0]
    dx = x.shape[1] // heads
    r = w.shape[1]
    dw = w.shape[2] // (2 * heads)
    do = r if trans_w else dw
    return pl.pallas_call(
        functools.partial(_headmm_kernel, trans_w=trans_w),
        grid=(heads,),
        in_specs=[pl.BlockSpec((m, dx), lambda h: (0, h)),
                  pl.BlockSpec((None, r, dw), lambda h: (layer, 0, 2 * h + w_off))],
        out_specs=pl.BlockSpec((m, do), lambda h: (0, h)),
        out_shape=jax.ShapeDtypeStruct((m, heads * do), out_dtype),
        compiler_params=_cp("parallel"),
        name="head_mm",
    )(x, w)


def _mod_kernel(c_ref, w_ref, b_ref, o_ref):
    c = c_ref[...]
    a = (c * jax.nn.sigmoid(c)).astype(BF16)
    o_ref[0] = jnp.dot(a, w_ref[0].astype(BF16), preferred_element_type=F32) + b_ref[0]


def _modulation_all(c_all, ada_w, ada_b):
    r, d = c_all.shape
    depth = ada_w.shape[0]
    s = depth * 2
    n3 = ada_w.shape[-1]
    w = ada_w.reshape(s, d, n3)
    b = ada_b.reshape(s, 1, n3)
    tn = _pick(n3, (512, 256, 128))
    return pl.pallas_call(
        _mod_kernel,
        grid=(s, n3 // tn),
        in_specs=[pl.BlockSpec((r, d), lambda i, j: (0, 0)),
                  pl.BlockSpec((1, d, tn), lambda i, j: (i, 0, j)),
                  pl.BlockSpec((1, 1, tn), lambda i, j: (i, 0, j))],
        out_specs=pl.BlockSpec((1, r, tn), lambda i, j: (i, 0, j)),
        out_shape=jax.ShapeDtypeStruct((s, r, n3), F32),
        compiler_params=_cp("parallel", "arbitrary"),
        name="modulation",
    )(c_all, w, b)


def _ln_mod_kernel(*refs, alpha, do_ln, do_mod):
    it = iter(refs)
    x_ref = next(it)
    if do_ln:
        f_ref, gate_ref, g_ref, b_ref = next(it), next(it), next(it), next(it)
    if do_mod:
        scale_ref, shift_ref = next(it), next(it)
    if do_ln:
        y_ref = next(it)
    if do_mod:
        h_ref = next(it)
    y = x_ref[0]
    if do_ln:
        xf = alpha * y + (1.0 + gate_ref[0]) * f_ref[0]
        mu = jnp.mean(xf, axis=-1, keepdims=True)
        xc = xf - mu
        var = jnp.mean(xc * xc, axis=-1, keepdims=True)
        y = xc * lax.rsqrt(var + LN_EPS) * g_ref[...] + b_ref[...]
        y_ref[0] = y
    if do_mod:
        h_ref[0] = (y * (1.0 + scale_ref[0]) + shift_ref[0]).astype(BF16)


def _ln_mod(x, f=None, gate=None, g=None, b=None, scale=None, shift=None, *, alpha):
    n, t, d = x.shape
    do_ln, do_mod = f is not None, scale is not None
    tt = _pick(t, (256, 128))
    xspec = pl.BlockSpec((1, tt, d), lambda i, j: (i, j, 0))

    def mspec(a):
        if a.shape[1] == 1:
            return pl.BlockSpec((1, 1, d), lambda i, j: (i, 0, 0))
        return xspec

    vspec = pl.BlockSpec((1, d), lambda i, j: (0, 0))
    args, specs, oshapes, ospecs = [x], [xspec], [], []
    if do_ln:
        args += [f, gate, g.reshape(1, d), b.reshape(1, d)]
        specs += [xspec, mspec(gate), vspec, vspec]
        oshapes.append(jax.ShapeDtypeStruct((n, t, d), F32))
        ospecs.append(xspec)
    if do_mod:
        args += [scale, shift]
        specs += [mspec(scale), mspec(shift)]
        oshapes.append(jax.ShapeDtypeStruct((n, t, d), BF16))
        ospecs.append(xspec)
    outs = pl.pallas_call(
        functools.partial(_ln_mod_kernel, alpha=alpha, do_ln=do_ln, do_mod=do_mod),
        grid=(n, t // tt),
        in_specs=specs, out_specs=ospecs, out_shape=oshapes,
        compiler_params=_cp("parallel", "parallel"),
        name="ln_mod",
    )(*args)
    return outs if len(outs) > 1 else outs[0]


def _s5_discretize(lam_re, lam_im, log_step, b_re, b_im):
    lr, li = lam_re.astype(F32), lam_im.astype(F32)
    dt = jnp.exp(log_step.astype(F32))[:, None]
    mag, ang = jnp.exp(lr * dt), li * dt
    a_re, a_im = mag * jnp.cos(ang), mag * jnp.sin(ang)
    den = lr * lr + li * li
    f_re = ((a_re - 1.0) * lr + a_im * li) / den
    f_im = (a_im * lr - (a_re - 1.0) * li) / den
    br, bi = b_re.astype(F32), b_im.astype(F32)
    bb_re = f_re[..., None] * br - f_im[..., None] * bi
    bb_im = f_re[..., None] * bi + f_im[..., None] * br
    return lr * dt, li * dt, a_re, a_im, bb_re, bb_im


def _s5_powers(ldt_re, ldt_im, ks):
    k = jnp.asarray(ks, F32)[:, None, None]
    mag, ang = jnp.exp(ldt_re * k), ldt_im * k
    return mag * jnp.cos(ang), mag * jnp.sin(ang)


def _block_diag(x, gl):
    *lead, g, a, b = x.shape
    q = g // gl
    xq = x.reshape(*lead, q, gl, a, b)
    eye = jnp.eye(gl, dtype=x.dtype)
    out = xq[..., :, :, :, None, :] * eye[:, None, :, None]
    return out.reshape(*lead, q, gl * a, gl * b)


def _s5_operators(prm, chunk, n_chunks):
    lam_re, lam_im, log_step, b_re, b_im, c_re, c_im = prm
    g, p = lam_re.shape
    c = b_re.shape[-1]
    gl = LANE // c
    q = g // gl
    ldr, ldi, _, _, bb_re, bb_im = _s5_discretize(lam_re, lam_im, log_step, b_re, b_im)
    cr, ci = c_re.astype(F32), c_im.astype(F32)
    ak_re, ak_im = _s5_powers(ldr, ldi, list(range(chunk + 1)))
    m_re = ak_re[..., None] * bb_re - ak_im[..., None] * bb_im
    m_im = ak_re[..., None] * bb_im + ak_im[..., None] * bb_re
    kk = (jnp.einsum("gop,kgpi->kgio", cr, m_re[:chunk], precision=HI)
          - jnp.einsum("gop,kgpi->kgio", ci, m_im[:chunk], precision=HI))
    kc = kk.reshape(chunk, q, gl * c, c).transpose(1, 0, 2, 3)
    rev = jnp.arange(chunk - 1, -1, -1)
    mc = jnp.stack([m_re[rev], m_im[rev]]).transpose(0, 1, 2, 4, 3)
    mc = mc.reshape(2, chunk, q, gl * c, p).transpose(2, 0, 1, 3, 4)
    pr, pi = ak_re[1:], ak_im[1:]
    v_re = cr[None] * pr[:, :, None, :] - ci[None] * pi[:, :, None, :]
    v_im = -cr[None] * pi[:, :, None, :] - ci[None] * pr[:, :, None, :]
    vc = jnp.stack([v_re, v_im]).transpose(0, 1, 2, 4, 3)
    vc = vc.reshape(2, chunk, q, gl * p, c).transpose(2, 0, 1, 3, 4)
    nsteps = max(1, int(math.ceil(math.log2(max(n_chunks, 2)))))
    ad_re, ad_im = _s5_powers(ldr, ldi, [chunk * (1 << i) for i in range(nsteps)])
    ad = jnp.concatenate([ad_re.reshape(nsteps, q, gl * p), ad_im.reshape(nsteps, q, gl * p)], axis=-1)
    rows = -(-nsteps // SUBLANE) * SUBLANE
    ad = jnp.pad(ad.transpose(1, 0, 2), ((0, 0), (0, rows - nsteps), (0, 0)))
    return kc, mc, vc, ad, nsteps


def _s5_step_operators(prm):
    lam_re, lam_im, log_step, b_re, b_im, c_re, c_im = prm
    g, p = lam_re.shape
    gl = LANE // b_re.shape[-1]
    q = g // gl
    _, _, a_re, a_im, bb_re, bb_im = _s5_discretize(lam_re, lam_im, log_step, b_re, b_im)
    bb = jnp.concatenate([_block_diag(bb_re.transpose(0, 2, 1), gl),
                          _block_diag(bb_im.transpose(0, 2, 1), gl)], axis=-1).astype(BF16)
    cr, ci = c_re.astype(F32), c_im.astype(F32)
    cc = jnp.concatenate([_block_diag(cr.transpose(0, 2, 1), gl),
                          _block_diag(-ci.transpose(0, 2, 1), gl)], axis=1).astype(BF16)
    a = jnp.concatenate([a_re.reshape(q, 1, gl * p), a_im.reshape(q, 1, gl * p)], axis=-1)
    return bb, cc, a


def _log2(x):
    assert x & (x - 1) == 0, "power of two expected"
    return x.bit_length() - 1


def _lane_tiler(src, dst):
    r = lax.broadcasted_iota(jnp.int32, (src, dst), 0)
    c = lax.broadcasted_iota(jnp.int32, (src, dst), 1)
    return ((c & (src - 1)) == r).astype(BF16)


def _same_block(rows, rblk, cols, cblk):
    r = lax.shift_right_logical(lax.broadcasted_iota(jnp.int32, (rows, cols), 0), _log2(rblk))
    c = lax.shift_right_logical(lax.broadcasted_iota(jnp.int32, (rows, cols), 1), _log2(cblk))
    return r == c


def _s5_scan_kernel(u_ref, kc_ref, mc_ref, vc_ref, ad_ref, d_ref, y_ref, sfin_ref, t_scr, w_scr, v_scr,
                    *, n, nk, nsteps, chunk):
    c = kc_ref.shape[-1]
    p = mc_ref.shape[-1]
    sw = (LANE // c) * p
    tile_c, tile_p = _lane_tiler(c, LANE), _lane_tiler(p, sw)
    mask_t = _same_block(LANE, c, LANE, c)
    mask_w = _same_block(LANE, c, sw, p)
    mask_v = _same_block(sw, p, LANE, c)

    def expand(x, tiler, mask):
        return jnp.where(mask, jnp.dot(x.astype(BF16), tiler, preferred_element_type=F32), 0.0).astype(BF16)

    bd = [expand(kc_ref[0, k], tile_c, mask_t) for k in range(chunk)]
    zero_blk = jnp.zeros((LANE, LANE), BF16)
    for ti in range(chunk):
        rows = slice(ti * LANE, (ti + 1) * LANE)
        for to in range(chunk):
            t_scr[rows, to * LANE:(to + 1) * LANE] = bd[to - ti] if to >= ti else zero_blk
        for ri in range(2):
            w_scr[rows, ri * sw:(ri + 1) * sw] = expand(mc_ref[0, ri, ti], tile_p, mask_w)
    for ri in range(2):
        for to in range(chunk):
            v_scr[ri * sw:(ri + 1) * sw, to * LANE:(to + 1) * LANE] = expand(vc_ref[0, ri, to], tile_c, mask_v)

    u = u_ref[0]
    ub = u.astype(BF16)
    x = jnp.dot(ub, w_scr[...], preferred_element_type=F32)
    half = x.shape[1] // 2
    xr, xi = x[:, :half], x[:, half:]
    nrows = x.shape[0]
    k = lax.rem(lax.broadcasted_iota(jnp.int32, (nrows, 1), 0), nk)
    for i in range(nsteps):
        d = 1 << i
        if d >= nk:
            break
        ar, ai = ad_ref[0, i:i + 1, :half], ad_ref[0, i:i + 1, half:]
        keep = k >= d
        sr = jnp.where(keep, pltpu.roll(xr, d, 0), 0.0)
        si = jnp.where(keep, pltpu.roll(xi, d, 0), 0.0)
        xr, xi = xr + (ar * sr - ai * si), xi + (ar * si + ai * sr)
    keep = k >= 1
    s0 = jnp.concatenate([jnp.where(keep, pltpu.roll(xr, 1, 0), 0.0),
                          jnp.where(keep, pltpu.roll(xi, 1, 0), 0.0)], axis=1).astype(BF16)
    y = jnp.dot(ub, t_scr[...], preferred_element_type=F32)
    y = y + jnp.dot(s0, v_scr[...], preferred_element_type=F32)
    y = y + d_ref[0] * u
    y_ref[0] = jax.nn.gelu(y).astype(BF16)
    for s in range(n):
        last = (s + 1) * nk - 1
        sfin_ref[0, s:s + 1, :] = jnp.concatenate([xr[last:last + 1], xi[last:last + 1]], axis=1)


def _s5_prompt(h, prm, w_in, d_skip, w_glu, layer):
    n, t, dm = h.shape
    g, p = prm[0].shape
    c = prm[3].shape[-1]
    gl = LANE // c
    q = g // gl
    chunk = S5_CHUNK
    nk = t // chunk
    u = _mm(h.reshape(n * t, dm), w_in, layer=layer)
    uc = u.reshape(n, nk, chunk, q, LANE).transpose(3, 0, 1, 2, 4).reshape(q, n * nk, chunk * LANE)
    kc, mc, vc, ad, nsteps = _s5_operators(prm, chunk, nk)
    d_t = jnp.tile(d_skip.astype(F32).reshape(q, 1, LANE), (1, 1, chunk))
    width = chunk * LANE
    sw = 2 * gl * p
    yc, sfin = pl.pallas_call(
        functools.partial(_s5_scan_kernel, n=n, nk=nk, nsteps=nsteps, chunk=chunk),
        grid=(q,),
        in_specs=[pl.BlockSpec((1, n * nk, width), lambda i: (i, 0, 0)),
                  pl.BlockSpec((1, chunk, LANE, c), lambda i: (i, 0, 0, 0)),
                  pl.BlockSpec((1, 2, chunk, LANE, p), lambda i: (i, 0, 0, 0, 0)),
                  pl.BlockSpec((1, 2, chunk, gl * p, c), lambda i: (i, 0, 0, 0, 0)),
                  pl.BlockSpec((1, ad.shape[1], sw), lambda i: (i, 0, 0)),
                  pl.BlockSpec((1, 1, width), lambda i: (i, 0, 0))],
        out_specs=[pl.BlockSpec((1, n * nk, width), lambda i: (i, 0, 0)),
                   pl.BlockSpec((1, n, sw), lambda i: (i, 0, 0))],
        out_shape=[jax.ShapeDtypeStruct((q, n * nk, width), BF16),
                   jax.ShapeDtypeStruct((q, n, sw), F32)],
        scratch_shapes=[pltpu.VMEM((width, width), BF16), pltpu.VMEM((width, sw), BF16),
                        pltpu.VMEM((sw, width), BF16)],
        compiler_params=_cp("parallel"),
        name="s5_scan",
    )(uc, kc, mc, vc, ad, d_t)
    y = yc.reshape(q, n, nk, chunk, LANE).transpose(1, 2, 3, 0, 4).reshape(n * t, dm)
    out = _mm_glu(y, w_glu, layer).reshape(n, t, dm)
    sf = sfin.reshape(q, n, 2, gl, p).transpose(2, 1, 0, 3, 4).reshape(2, n, g, p)
    return out, sf[0], sf[1]


def _s5_step_kernel(u_ref, sr_ref, si_ref, bb_ref, cc_ref, a_ref, d_ref, y_ref, or_ref, oi_ref):
    u = u_ref[...]
    bu = jnp.dot(u.astype(BF16), bb_ref[0], preferred_element_type=F32)
    half = bu.shape[1] // 2
    ar, ai = a_ref[0, :, :half], a_ref[0, :, half:]
    sr, si = sr_ref[...], si_ref[...]
    nr = bu[:, :half] + (ar * sr - ai * si)
    ni = bu[:, half:] + (ar * si + ai * sr)
    or_ref[...] = nr
    oi_ref[...] = ni
    s = jnp.concatenate([nr, ni], axis=1).astype(BF16)
    y = jnp.dot(s, cc_ref[0], preferred_element_type=F32) + d_ref[0] * u
    y_ref[...] = jax.nn.gelu(y).astype(BF16)


def _s5_sample(h, s0_re, s0_im, prm, w_in, d_skip, w_glu, layer):
    n, _, dm = h.shape
    g, p = prm[0].shape
    gl = LANE // prm[3].shape[-1]
    q = g // gl
    sw = gl * p
    u = _mm(h.reshape(n, dm), w_in, layer=layer)
    bb, cc, a = _s5_step_operators(prm)
    y, nr, ni = pl.pallas_call(
        _s5_step_kernel,
        grid=(q,),
        in_specs=[pl.BlockSpec((n, LANE), lambda i: (0, i)),
                  pl.BlockSpec((n, sw), lambda i: (0, i)),
                  pl.BlockSpec((n, sw), lambda i: (0, i)),
                  pl.BlockSpec((1, LANE, 2 * sw), lambda i: (i, 0, 0)),
                  pl.BlockSpec((1, 2 * sw, LANE), lambda i: (i, 0, 0)),
                  pl.BlockSpec((1, 1, 2 * sw), lambda i: (i, 0, 0)),
                  pl.BlockSpec((1, 1, LANE), lambda i: (i, 0, 0))],
        out_specs=[pl.BlockSpec((n, LANE), lambda i: (0, i)),
                   pl.BlockSpec((n, sw), lambda i: (0, i)),
                   pl.BlockSpec((n, sw), lambda i: (0, i))],
        out_shape=[jax.ShapeDtypeStruct((n, dm), BF16),
                   jax.ShapeDtypeStruct((n, g * p), F32),
                   jax.ShapeDtypeStruct((n, g * p), F32)],
        compiler_params=_cp("parallel"),
        name="s5_step",
    )(u, s0_re.astype(F32).reshape(n, g * p), s0_im.astype(F32).reshape(n, g * p), bb, cc, a,
      d_skip.astype(F32).reshape(q, 1, LANE))
    out = _mm_glu(y, w_glu, layer).reshape(n, 1, dm)
    return out, nr.reshape(n, g, p), ni.reshape(n, g, p)


def _tri_lower(nrows, ncols, strict):
    r = lax.broadcasted_iota(jnp.int32, (nrows, ncols), 0)
    c = lax.broadcasted_iota(jnp.int32, (nrows, ncols), 1)
    return ((c < r) if strict else (c <= r)).astype(F32)


def _cumsum_kernel(x_ref, o_ref, *, blk):
    t = x_ref.shape[1]
    tri = _tri_lower(blk, blk, strict=False)
    carry = jnp.zeros((1, x_ref.shape[2]), F32)
    for i in range(t // blk):
        xb = x_ref[0, i * blk:(i + 1) * blk, :]
        cb = jnp.dot(tri, xb, preferred_element_type=F32, precision=HI) + carry
        o_ref[0, i * blk:(i + 1) * blk, :] = cb
        carry = cb[blk - 1:blk, :]


def _cumsum_time(x):
    n, t, hh = x.shape
    blk = _pick(t, (128,))
    return pl.pallas_call(
        functools.partial(_cumsum_kernel, blk=blk),
        grid=(n,),
        in_specs=[pl.BlockSpec((1, t, hh), lambda i: (i, 0, 0))],
        out_specs=pl.BlockSpec((1, t, hh), lambda i: (i, 0, 0)),
        out_shape=jax.ShapeDtypeStruct((n, t, hh), F32),
        compiler_params=_cp("parallel"),
        name="cumsum_time",
    )(x)


def _softmax_update(s, m_prev, l_prev):
    m_new = jnp.maximum(m_prev, jnp.max(s, axis=-1, keepdims=True))
    corr = jnp.exp(m_prev - m_new)
    p = jnp.exp(s - m_new)
    return m_new, corr, p, l_prev * corr + jnp.sum(p, axis=-1, keepdims=True)


def _fox_attn_kernel(q_ref, k_ref, v_ref, fq_ref, fk_ref, o_ref, *, gq, tq, scale):
    g = lax.rem(pl.program_id(1), gq)
    qi = pl.program_id(2)
    q = q_ref[0]
    lane = lax.broadcasted_iota(jnp.int32, fq_ref.shape[2:], 1)
    fq = jnp.sum(jnp.where(lane == g, fq_ref[0, 0], 0.0), axis=1, keepdims=True)
    rowpos = qi * tq + lax.broadcasted_iota(jnp.int32, (tq, 1), 0)

    def kv_step(j, carry):
        m_prev, l_prev, acc = carry
        start = pl.multiple_of(j * tq, tq)
        s = lax.dot_general(q, k_ref[0, pl.ds(start, tq), :], NT, preferred_element_type=F32) * scale
        s = s + (fq - fk_ref[0, 0, j, pl.ds(g, 1), :])
        causal = rowpos >= (j * tq + lax.broadcasted_iota(jnp.int32, (1, tq), 1))
        s = jnp.where(causal, s, -jnp.inf)
        m_new, corr, p, l_new = _softmax_update(s, m_prev, l_prev)
        acc = acc * corr + jnp.dot(p.astype(BF16), v_ref[0, pl.ds(start, tq), :], preferred_element_type=F32)
        return m_new, l_new, acc

    init = (jnp.full((tq, 1), -jnp.inf, F32), jnp.zeros((tq, 1), F32), jnp.zeros((tq, v_ref.shape[2]), F32))
    _, l, acc = lax.fori_loop(0, qi + 1, kv_step, init)
    o_ref[0] = (acc / l).astype(o_ref.dtype)


def _fox_prompt_attention(q, kvb, f_cum, kvh, hd):
    n, t, qd = q.shape
    heads = qd // hd
    gq = heads // kvh
    tq = _pick(t, (512, 256, 128))
    nq = t // tq
    fq = f_cum.reshape(n, t, kvh, gq).transpose(0, 2, 1, 3)
    fk = f_cum.reshape(n, nq, tq, kvh, gq).transpose(0, 3, 1, 4, 2)
    return pl.pallas_call(
        functools.partial(_fox_attn_kernel, gq=gq, tq=tq, scale=hd ** -0.5),
        grid=(n, heads, nq),
        in_specs=[pl.BlockSpec((1, tq, hd), lambda b, h, i: (b, i, h)),
                  pl.BlockSpec((1, t, hd), lambda b, h, i: (b, 0, h // gq)),
                  pl.BlockSpec((1, t, hd), lambda b, h, i: (b, 0, kvh + h // gq)),
                  pl.BlockSpec((1, 1, tq, gq), lambda b, h, i: (b, h // gq, i, 0)),
                  pl.BlockSpec((1, 1, nq, gq, tq), lambda b, h, i: (b, h // gq, 0, 0, 0))],
        out_specs=pl.BlockSpec((1, tq, hd), lambda b, h, i: (b, i, h)),
        out_shape=jax.ShapeDtypeStruct((n, t, qd), BF16),
        compiler_params=_cp("parallel", "parallel", "arbitrary"),
        name="fox_attn",
    )(q, kvb, kvb, fq, fk)


def _mla_attn_kernel(qn_ref, qr_ref, kn_ref, kr_ref, v_ref, o_ref, *, tq, scale):
    qi = pl.program_id(2)
    qn = qn_ref[0]
    qr = qr_ref[0, 0]
    rowpos = qi * tq + lax.broadcasted_iota(jnp.int32, (tq, 1), 0)

    def kv_step(j, carry):
        m_prev, l_prev, acc = carry
        start = pl.multiple_of(j * tq, tq)
        s = lax.dot_general(qn, kn_ref[0, pl.ds(start, tq), :], NT, preferred_element_type=F32)
        s = (s + lax.dot_general(qr, kr_ref[0, pl.ds(start, tq), :], NT, preferred_element_type=F32)) * scale
        causal = rowpos >= (j * tq + lax.broadcasted_iota(jnp.int32, (1, tq), 1))
        s = jnp.where(causal, s, -jnp.inf)
        m_new, corr, p, l_new = _softmax_update(s, m_prev, l_prev)
        acc = acc * corr + jnp.dot(p.astype(BF16), v_ref[0, pl.ds(start, tq), :], preferred_element_type=F32)
        return m_new, l_new, acc

    init = (jnp.full((tq, 1), -jnp.inf, F32), jnp.zeros((tq, 1), F32), jnp.zeros((tq, v_ref.shape[2]), F32))
    _, l, acc = lax.fori_loop(0, qi + 1, kv_step, init)
    o_ref[0] = (acc / l).astype(o_ref.dtype)


def _mla_prompt_attention(qn, qr, kvb, kr, heads, dn, dr, dv):
    n, t, _ = qn.shape
    tq = _pick(t, (512, 256, 128))
    nq = t // tq
    return pl.pallas_call(
        functools.partial(_mla_attn_kernel, tq=tq, scale=(dn + dr) ** -0.5),
        grid=(n, heads, nq),
        in_specs=[pl.BlockSpec((1, tq, dn), lambda b, h, i: (b, i, h)),
                  pl.BlockSpec((1, 1, tq, dr), lambda b, h, i: (b, h, i, 0)),
                  pl.BlockSpec((1, t, dn), lambda b, h, i: (b, 0, 2 * h)),
                  pl.BlockSpec((1, t, dr), lambda b, h, i: (b, 0, 0)),
                  pl.BlockSpec((1, t, dv), lambda b, h, i: (b, 0, 2 * h + 1))],
        out_specs=pl.BlockSpec((1, tq, dv), lambda b, h, i: (b, i, h)),
        out_shape=jax.ShapeDtypeStruct((n, t, heads * dv), BF16),
        compiler_params=_cp("parallel", "parallel", "arbitrary"),
        name="mla_attn",
    )(qn, qr, kvb, kr, kvb)


def _fox_decode_kernel(pt_ref, *refs, pp, kvh, gq, hd, scale):
    del pt_ref
    q_ref, knew_ref, vnew_ref, cnew_ref = refs[:4]
    k_refs = refs[4:4 + pp]
    v_refs = refs[4 + pp:4 + 2 * pp]
    f_refs = refs[4 + 2 * pp:4 + 3 * pp]
    o_ref = refs[4 + 3 * pp]
    m_ref, l_ref, acc_ref, carry_ref = refs[5 + 3 * pp:]
    step = pl.program_id(1)
    ps = f_refs[0].shape[2]

    @pl.when(step == 0)
    def _():
        m_ref[...] = jnp.full(m_ref.shape, -jnp.inf, F32)
        l_ref[...] = jnp.zeros(l_ref.shape, F32)
        acc_ref[...] = jnp.zeros(acc_ref.shape, F32)
        carry_ref[...] = jnp.zeros(carry_ref.shape, F32)

    q = q_ref[0]
    cnew = cnew_ref[0]
    later = (lax.broadcasted_iota(jnp.int32, (ps, ps), 0) > lax.broadcasted_iota(jnp.int32, (ps, ps), 1)).astype(F32)
    carry = carry_ref[...]
    scores = []
    for i in range(pp):
        lf = f_refs[i][0]
        after = jnp.dot(lf, later, preferred_element_type=F32, precision=HI) + carry
        carry = carry + jnp.sum(lf, axis=1, keepdims=True)
        rows = []
        for kk in range(kvh):
            kb = k_refs[i][0, pl.ds(kk, ps, stride=kvh), :].astype(BF16)
            rows.append(lax.dot_general(q[kk * gq:(kk + 1) * gq], kb, NT, preferred_element_type=F32))
        scores.append(jnp.concatenate(rows, axis=0) * scale + (cnew + after))
    carry_ref[...] = carry
    s_all = jnp.concatenate(scores, axis=1)
    m_new, corr, p_all, l_new = _softmax_update(s_all, m_ref[...], l_ref[...])
    m_ref[...] = m_new
    l_ref[...] = l_new
    pb = p_all.astype(BF16)
    outs = []
    for kk in range(kvh):
        part = jnp.zeros((gq, hd), F32)
        for i in range(pp):
            vb = v_refs[i][0, pl.ds(kk, ps, stride=kvh), :].astype(BF16)
            part = part + jnp.dot(pb[kk * gq:(kk + 1) * gq, i * ps:(i + 1) * ps], vb,
                                  preferred_element_type=F32)
        outs.append(part)
    acc_ref[...] = acc_ref[...] * corr + jnp.concatenate(outs, axis=0)

    @pl.when(step == pl.num_programs(1) - 1)
    def _():
        qf = q.astype(F32)
        kn = knew_ref[0].astype(BF16).astype(F32)
        vn = vnew_ref[0].astype(BF16).astype(F32)
        kx = jnp.concatenate([jnp.broadcast_to(kn[kk:kk + 1], (gq, hd)) for kk in range(kvh)], axis=0)
        vx = jnp.concatenate([jnp.broadcast_to(vn[kk:kk + 1], (gq, hd)) for kk in range(kvh)], axis=0)
        s_new = jnp.sum(qf * kx, axis=1, keepdims=True) * scale + (cnew - cnew)
        m_fin, corr_fin, p_new, l_fin = _softmax_update(s_new, m_ref[...], l_ref[...])
        acc = acc_ref[...] * corr_fin + p_new.astype(BF16).astype(F32) * vx
        o_ref[0] = acc / l_fin


def _fox_sample_attention(q, k_new, v_new, logf_new, cache_k, cache_v, cache_logf, layer, page_table):
    n = q.shape[0]
    _, pool, ps, kvh, hd = cache_k.shape
    hh = cache_logf.shape[-1]
    gq = hh // kvh
    npg = page_table.shape[1]
    pp = _pick(npg, (16, 8, 4, 2, 1))
    ck = cache_k.reshape(-1, ps * kvh, hd)
    cv = cache_v.reshape(-1, ps * kvh, hd)
    cf = cache_logf.reshape(-1, ps, hh).transpose(0, 2, 1)
    base = layer * pool

    def page_map(i):
        return lambda s, p, pt: (base + pt[s, npg - 1 - (p * pp + i)], 0, 0)

    seq3 = lambda s, p, pt: (s, 0, 0)
    in_specs = [pl.BlockSpec((1, hh, hd), seq3),
                pl.BlockSpec((1, kvh, hd), seq3),
                pl.BlockSpec((1, kvh, hd), seq3),
                pl.BlockSpec((1, hh, 1), seq3)]
    in_specs += [pl.BlockSpec((1, ps * kvh, hd), page_map(i)) for i in range(pp)]
    in_specs += [pl.BlockSpec((1, ps * kvh, hd), page_map(i)) for i in range(pp)]
    in_specs += [pl.BlockSpec((1, hh, ps), page_map(i)) for i in range(pp)]
    out = pl.pallas_call(
        functools.partial(_fox_decode_kernel, pp=pp, kvh=kvh, gq=gq, hd=hd, scale=hd ** -0.5),
        grid_spec=pltpu.PrefetchScalarGridSpec(
            num_scalar_prefetch=1,
            grid=(n, npg // pp),
            in_specs=in_specs,
            out_specs=pl.BlockSpec((1, hh, hd), seq3),
            scratch_shapes=[pltpu.VMEM((hh, 1), F32), pltpu.VMEM((hh, 1), F32),
                            pltpu.VMEM((hh, hd), F32), pltpu.VMEM((hh, 1), F32)]),
        out_shape=jax.ShapeDtypeStruct((n, hh, hd), F32),
        compiler_params=_cp("parallel", "arbitrary"),
        name="fox_decode",
    )(page_table, q.reshape(n, hh, hd).astype(BF16), k_new.reshape(n, kvh, hd), v_new.reshape(n, kvh, hd),
      logf_new.reshape(n, hh, 1), *([ck] * pp), *([cv] * pp), *([cf] * pp))
    return out.reshape(n, hh * hd).astype(BF16)


def _mla_decode_kernel(pt_ref, *refs, pp, scale):
    del pt_ref
    ql_ref, qr_ref, cnew_ref, rnew_ref = refs[:4]
    c_refs = refs[4:4 + pp]
    r_refs = refs[4 + pp:4 + 2 * pp]
    o_ref = refs[4 + 2 * pp]
    m_ref, l_ref, acc_ref = refs[5 + 2 * pp:]
    step = pl.program_id(1)
    ps = c_refs[0].shape[1]

    @pl.when(step == 0)
    def _():
        m_ref[...] = jnp.full(m_ref.shape, -jnp.inf, F32)
        l_ref[...] = jnp.zeros(l_ref.shape, F32)
        acc_ref[...] = jnp.zeros(acc_ref.shape, F32)

    ql = ql_ref[0]
    qr = qr_ref[0]
    cbs, scores = [], []
    for i in range(pp):
        cb = c_refs[i][0].astype(BF16)
        rb = r_refs[i][0].astype(BF16)
        s = lax.dot_general(ql, cb, NT, preferred_element_type=F32)
        scores.append((s + lax.dot_general(qr, rb, NT, preferred_element_type=F32)) * scale)
        cbs.append(cb)
    s_all = jnp.concatenate(scores, axis=1)
    m_new, corr, p_all, l_new = _softmax_update(s_all, m_ref[...], l_ref[...])
    m_ref[...] = m_new
    l_ref[...] = l_new
    pb = p_all.astype(BF16)
    acc = acc_ref[...] * corr
    for i in range(pp):
        acc = acc + jnp.dot(pb[:, i * ps:(i + 1) * ps], cbs[i], preferred_element_type=F32)
    acc_ref[...] = acc

    @pl.when(step == pl.num_programs(1) - 1)
    def _():
        cn = cnew_ref[0].astype(BF16).astype(F32)
        rn = rnew_ref[0].astype(BF16).astype(F32)
        s_new = (jnp.sum(ql.astype(F32) * cn, axis=1, keepdims=True)
                 + jnp.sum(qr.astype(F32) * rn, axis=1, keepdims=True)) * scale
        m_fin, corr_fin, p_new, l_fin = _softmax_update(s_new, m_ref[...], l_ref[...])
        acc_fin = acc_ref[...] * corr_fin + p_new.astype(BF16).astype(F32) * cn
        o_ref[0] = (acc_fin / l_fin).astype(o_ref.dtype)


def _mla_sample_attention(q_lat, q_rot, ckv_new, kpe_new, cache_ckv, cache_kpe, layer, page_table, heads, scale):
    n = q_lat.shape[0]
    _, pool, ps, kl = cache_ckv.shape
    rr = cache_kpe.shape[-1]
    npg = page_table.shape[1]
    pp = _pick(npg, (16, 8, 4, 2, 1))
    cc = cache_ckv.reshape(-1, ps, kl)
    cr = cache_kpe.reshape(-1, ps, rr)
    base = layer * pool

    def page_map(i):
        return lambda s, p, pt: (base + pt[s, p * pp + i], 0, 0)

    seq3 = lambda s, p, pt: (s, 0, 0)
    in_specs = [pl.BlockSpec((1, heads, kl), seq3), pl.BlockSpec((1, heads, rr), seq3),
                pl.BlockSpec((1, 1, kl), seq3), pl.BlockSpec((1, 1, rr), seq3)]
    in_specs += [pl.BlockSpec((1, ps, kl), page_map(i)) for i in range(pp)]
    in_specs += [pl.BlockSpec((1, ps, rr), page_map(i)) for i in range(pp)]
    out = pl.pallas_call(
        functools.partial(_mla_decode_kernel, pp=pp, scale=scale),
        grid_spec=pltpu.PrefetchScalarGridSpec(
            num_scalar_prefetch=1,
            grid=(n, npg // pp),
            in_specs=in_specs,
            out_specs=pl.BlockSpec((1, heads, kl), seq3),
            scratch_shapes=[pltpu.VMEM((heads, 1), F32), pltpu.VMEM((heads, 1), F32),
                            pltpu.VMEM((heads, kl), F32)]),
        out_shape=jax.ShapeDtypeStruct((n, heads, kl), BF16),
        compiler_params=_cp("parallel", "arbitrary"),
        name="mla_decode",
    )(page_table, q_lat.reshape(n, heads, kl), q_rot.reshape(n, heads, rr),
      ckv_new.reshape(n, 1, kl), kpe_new.reshape(n, 1, rr), *([cc] * pp), *([cr] * pp))
    return out.reshape(n, heads * kl)


def _mla_norm_kernel(z_ref, gq_ref, gkv_ref, cq_ref, ckv_ref, ckvb_ref, *, ql):
    z = z_ref[...]
    zq, zk = z[:, :ql], z[:, ql:]
    cq = zq * lax.rsqrt(jnp.mean(zq * zq, axis=-1, keepdims=True) + RMS_EPS) * gq_ref[...]
    ck = zk * lax.rsqrt(jnp.mean(zk * zk, axis=-1, keepdims=True) + RMS_EPS) * gkv_ref[...]
    cq_ref[...] = cq.astype(BF16)
    ckv_ref[...] = ck
    ckvb_ref[...] = ck.astype(BF16)


def _mla_norm(z, g_q, g_kv):
    m, w = z.shape
    ql = g_q.shape[0]
    kl = g_kv.shape[0]
    tm = _pick(m, (512, 256, 128))
    return pl.pallas_call(
        functools.partial(_mla_norm_kernel, ql=ql),
        grid=(m // tm,),
        in_specs=[pl.BlockSpec((tm, w), lambda i: (i, 0)),
                  pl.BlockSpec((1, ql), lambda i: (0, 0)),
                  pl.BlockSpec((1, kl), lambda i: (0, 0))],
        out_specs=[pl.BlockSpec((tm, ql), lambda i: (i, 0)),
                   pl.BlockSpec((tm, kl), lambda i: (i, 0)),
                   pl.BlockSpec((tm, kl), lambda i: (i, 0))],
        out_shape=[jax.ShapeDtypeStruct((m, ql), BF16), jax.ShapeDtypeStruct((m, kl), F32),
                   jax.ShapeDtypeStruct((m, kl), BF16)],
        compiler_params=_cp("parallel"),
        name="mla_norm",
    )(z, g_q.astype(F32).reshape(1, ql), g_kv.astype(F32).reshape(1, kl))


def _rope_tables(pos, half, reps):
    inv = ROPE_BASE ** (-jnp.arange(half, dtype=F32) / half)
    ang = pos[:, None] * inv
    cos, sin = jnp.cos(ang), jnp.sin(ang)
    return (jnp.tile(jnp.concatenate([cos, cos], axis=-1), (1, reps)),
            jnp.tile(jnp.concatenate([-sin, sin], axis=-1), (1, reps)))


def _swap_halves(w, width):
    k, n = w.shape
    return w.reshape(k, n // width, 2, width // 2)[:, :, ::-1, :].reshape(k, n)


def _sort_desc(v):
    v = list(v)
    n = len(v)
    k = 2
    while k <= n:
        j = k // 2
        while j >= 1:
            for i in range(n):
                l = i ^ j
                if l > i:
                    hi, lo = jnp.maximum(v[i], v[l]), jnp.minimum(v[i], v[l])
                    v[i], v[l] = (hi, lo) if (i & k) == 0 else (lo, hi)
            j //= 2
        k *= 2
    return v


def _merge_desc(v):
    v = list(v)
    j = len(v) // 2
    while j >= 1:
        for i in range(len(v)):
            l = i ^ j
            if l > i:
                v[i], v[l] = jnp.maximum(v[i], v[l]), jnp.minimum(v[i], v[l])
        j //= 2
    return v


def _top_of_two(a, b):
    n = len(a)
    return _merge_desc([jnp.maximum(a[i], b[n - 1 - i]) for i in range(n)])


def _fold_sublanes(v):
    shift = v[0].shape[0] // 2
    while shift >= 1:
        v = _top_of_two(v, [pltpu.roll(x, shift, 0) for x in v])
        shift //= 2
    return v


def _peer_route_kernel(qp_ref, sk_ref, s1_ref, c1_ref, s2_ref, e2_ref, tau_ref, *, heads, nkeys, topk):
    tt = qp_ref.shape[0]
    sub = lax.broadcasted_iota(jnp.int32, (SUBLANE, tt), 0)
    for h in range(heads):
        sts, vals = [], []
        for c in range(2):
            qh = qp_ref[:, (2 * h + c) * nkeys:(2 * h + c + 1) * nkeys]
            st = lax.dot_general(sk_ref[c, h], qh, NT, preferred_element_type=F32, precision=HI)
            sts.append(st)
            slabs = [st[SUBLANE * i:SUBLANE * (i + 1), :] for i in range(nkeys // SUBLANE)]
            vals.append(_fold_sublanes(_sort_desc(slabs)))
        v1, v2 = vals
        cands = []
        for half in range(topk // SUBLANE):
            b = v2[half * SUBLANE + SUBLANE - 1]
            for s in range(SUBLANE - 2, -1, -1):
                b = jnp.where(sub == s, v2[half * SUBLANE + s], b)
            cands.append([a + b for a in v1])
        top = cands[0]
        for other in cands[1:]:
            top = _top_of_two(top, other)
        top = _fold_sublanes(top)
        mx = top[0][0:1]
        z = jnp.zeros_like(mx)
        for r in range(topk):
            z = z + jnp.exp(top[r][0:1] - mx)
        s1_ref[h] = sts[0]
        s2_ref[h] = sts[1]
        c1_ref[h] = jnp.exp(sts[0] - v1[0][0:1]) / z
        e2_ref[h] = jnp.exp(sts[1] - v2[0][0:1])
        tau_ref[h:h + 1, :] = top[topk - 1][0:1]


def _peer_dense_kernel(h_ref, u_ref, v_ref, s1_ref, c1_ref, s2_ref, e2_ref, tau_ref, o_ref, ht_ref,
                       *, heads, nkeys):
    e = pl.program_id(1)
    eb = u_ref.shape[0]
    tt = h_ref.shape[0]
    per = eb // nkeys

    @pl.when(e == 0)
    def _():
        ht_ref[...] = h_ref[...].astype(F32).T.astype(BF16)
        o_ref[...] = jnp.zeros(o_ref.shape, F32)

    st = jnp.dot(u_ref[...], ht_ref[...], preferred_element_type=F32)
    act = jax.nn.gelu(st)
    parts = []
    for al in range(per):
        a = e * per + al
        w = jnp.zeros((nkeys, tt), F32)
        for h in range(heads):
            sc = s2_ref[h] + s1_ref[h, pl.ds(a, 1), :]
            w = w + jnp.where(sc >= tau_ref[h:h + 1, :], e2_ref[h] * c1_ref[h, pl.ds(a, 1), :], 0.0)
        parts.append(w * act[al * nkeys:(al + 1) * nkeys])
    pt = parts[0] if len(parts) == 1 else jnp.concatenate(parts, axis=0)
    o_ref[...] += jnp.dot(pt.T.astype(BF16), v_ref[...], preferred_element_type=F32)


def _peer_ffn(h, w_q, subkeys, u_tab, v_tab, layer):
    m, d = h.shape
    _, heads, nkeys, kd2 = subkeys.shape
    assert nkeys == LANE and kd2 == LANE, "product-key halves are laid out on one 128-lane tile"
    assert nkeys // SUBLANE == PEER_TOPK, "the top-k network keeps one value per slab of keys"
    ne = u_tab.shape[1]
    qp = _mm(h, w_q, layer=layer)
    tt = _pick(m, (256, 128))
    tab = jax.ShapeDtypeStruct((heads, nkeys, m), F32)
    tspec = pl.BlockSpec((heads, nkeys, tt), lambda i: (0, 0, i))
    s1, c1, s2, e2, tau = pl.pallas_call(
        functools.partial(_peer_route_kernel, heads=heads, nkeys=nkeys, topk=PEER_TOPK),
        grid=(m // tt,),
        in_specs=[pl.BlockSpec((tt, qp.shape[1]), lambda i: (i, 0)),
                  pl.BlockSpec((2, heads, nkeys, kd2), lambda i: (0, 0, 0, 0))],
        out_specs=[tspec, tspec, tspec, tspec, pl.BlockSpec((heads, tt), lambda i: (0, i))],
        out_shape=[tab, tab, tab, tab, jax.ShapeDtypeStruct((heads, m), F32)],
        compiler_params=_cp("parallel"),
        name="peer_route",
    )(qp, subkeys.astype(F32))
    td = _pick(m, (512, 256, 128))
    eb = _pick(ne, (512, 256, 128))
    once = pl.Buffered(1)
    dspec = pl.BlockSpec((heads, nkeys, td), lambda i, e: (0, 0, i), pipeline_mode=once)
    return pl.pallas_call(
        functools.partial(_peer_dense_kernel, heads=heads, nkeys=nkeys),
        grid=(m // td, ne // eb),
        in_specs=[pl.BlockSpec((td, d), lambda i, e: (i, 0), pipeline_mode=once),
                  pl.BlockSpec((None, eb, d), lambda i, e: (layer, e, 0)),
                  pl.BlockSpec((None, eb, d), lambda i, e: (layer, e, 0)),
                  dspec, dspec, dspec, dspec,
                  pl.BlockSpec((heads, td), lambda i, e: (0, i), pipeline_mode=once)],
        out_specs=pl.BlockSpec((td, d), lambda i, e: (i, 0)),
        out_shape=jax.ShapeDtypeStruct((m, d), F32),
        scratch_shapes=[pltpu.VMEM((d, td), BF16)],
        compiler_params=_cp("parallel", "arbitrary"),
        name="peer_dense",
    )(h, u_tab, v_tab, s1, c1, s2, e2, tau)


def kernel(x_prompt, x_sample, state_s5_re, state_s5_im, cache_fox_k, cache_fox_v, cache_fox_logf, cache_mla_ckv, cache_mla_kpe, page_table, c_prompt, c_sample, ada_w, ada_b, ln_g, ln_b, s5_w_in, s5_lam_re, s5_lam_im, s5_log_step, s5_b_re, s5_b_im, s5_c_re, s5_c_im, s5_d, s5_w_glu, fox_w_in, fox_b_f, fox_w_o, mla_w_in, mla_q_norm, mla_kv_norm, mla_w_uq, mla_w_ukv, mla_w_o, peer_w_q, peer_subkeys, peer_u, peer_v):
    n_p, t_p, dm = x_prompt.shape
    n_s, t_s, _ = x_sample.shape
    assert t_s == 1, "the sample group decodes one token per sequence"
    depth = ada_w.shape[0]
    alpha = (2 * depth) ** 0.25
    past_len = page_table.shape[1] * cache_fox_k.shape[2]
    fox_heads = fox_b_f.shape[-1]
    fox_kvh, fox_hd = cache_fox_k.shape[3], cache_fox_k.shape[4]
    mla_ql, mla_kl, mla_r = mla_q_norm.shape[-1], mla_kv_norm.shape[-1], cache_mla_kpe.shape[-1]
    mla_heads = (mla_w_uq.shape[-1] - mla_w_ukv.shape[-1] + mla_w_o.shape[-2]) // mla_r
    mla_dv = mla_w_o.shape[-2] // mla_heads
    mla_dn = mla_w_ukv.shape[-1] // mla_heads - mla_dv
    assert mla_dn == mla_dv, "k_nope and v column blocks of w_ukv are addressed with one block width"

    s5_w_in_b, s5_w_glu_b = s5_w_in.astype(BF16), s5_w_glu.astype(BF16)
    fox_w_in_b, fox_w_o_b = fox_w_in.astype(BF16), fox_w_o.astype(BF16)
    mla_w_in_b, mla_w_uq_b = mla_w_in.astype(BF16), mla_w_uq.astype(BF16)
    mla_w_ukv_b, mla_w_o_b = mla_w_ukv.astype(BF16), mla_w_o.astype(BF16)
    peer_w_q_b, peer_u_b, peer_v_b = peer_w_q.astype(BF16), peer_u.astype(BF16), peer_v.astype(BF16)

    rows = n_p + n_s
    rpad = -(-rows // SUBLANE) * SUBLANE
    c_all = jnp.pad(jnp.concatenate([c_prompt, c_sample], axis=0).astype(F32), ((0, rpad - rows), (0, 0)))
    mod = _modulation_all(c_all, ada_w.astype(F32), ada_b.astype(F32))

    def mods(i, s):
        m = mod[2 * i + s]
        mp = m[:n_p, None, :]
        ms = m[n_p:rows][None]
        cut = lambda a: (a[..., :dm], a[..., dm:2 * dm], a[..., 2 * dm:])
        return cut(mp), cut(ms)

    pos_p = jnp.arange(t_p, dtype=F32)
    pos_s = past_len + jnp.arange(t_s, dtype=F32)
    cos_p, sin_p = _rope_tables(pos_p, mla_r // 2, LANE // mla_r)
    cos_p, sin_p = jnp.tile(cos_p, (n_p, 1)), jnp.tile(sin_p, (n_p, 1))
    cos_s, sin_s = _rope_tables(jnp.tile(pos_s, (n_s,)), mla_r // 2, LANE // mla_r)

    yp, ys = x_prompt.astype(F32), x_sample.astype(F32).reshape(1, n_s, dm)
    (shp, scp, _), (shs, scs, _) = mods(0, 0)
    hp = _ln_mod(yp, scale=scp, shift=shp, alpha=alpha)
    hs = _ln_mod(ys, scale=scs, shift=shs, alpha=alpha)

    out = {k: [] for k in ("s5r_p", "s5i_p", "s5r_s", "s5i_s", "fk_p", "fv_p", "fl_p", "fk_s", "fv_s", "fl_s",
                           "ckv_p", "kpe_p", "ckv_s", "kpe_s")}
    for i in range(depth):
        kind, j = i % 3, i // 3
        (_, _, gate_p), (_, _, gate_s) = mods(i, 0)
        if kind == 0:
            prm = (s5_lam_re[j], s5_lam_im[j], s5_log_step[j], s5_b_re[j], s5_b_im[j], s5_c_re[j], s5_c_im[j])
            op, sr, si = _s5_prompt(hp, prm, s5_w_in_b, s5_d[j], s5_w_glu_b, j)
            out["s5r_p"].append(sr)
            out["s5i_p"].append(si)
            os_, sr, si = _s5_sample(hs.reshape(n_s, 1, dm), state_s5_re[j], state_s5_im[j], prm, s5_w_in_b,
                                     s5_d[j], s5_w_glu_b, j)
            out["s5r_s"].append(sr)
            out["s5i_s"].append(si)
            os_ = os_.reshape(1, n_s, dm)
        elif kind == 1:
            nq, nkv = fox_heads * fox_hd, fox_kvh * fox_hd
            w_f = fox_w_in_b[j][:, nq + 2 * nkv:]
            h2 = hp.reshape(n_p * t_p, dm)
            q = _mm(h2, fox_w_in_b, (BF16,), layer=j, col0=0, ncols=nq)
            kv, kvb = _mm(h2, fox_w_in_b, (F32, BF16), layer=j, col0=nq, ncols=2 * nkv)
            lf = _mm_logsig(h2, w_f, fox_b_f[j])
            f_cum = _cumsum_time(lf.reshape(n_p, t_p, fox_heads))
            o = _fox_prompt_attention(q.reshape(n_p, t_p, nq), kvb.reshape(n_p, t_p, 2 * nkv), f_cum,
                                      fox_kvh, fox_hd)
            op = _mm(o.reshape(n_p * t_p, nq), fox_w_o_b, layer=j).reshape(n_p, t_p, dm)
            out["fk_p"].append(kv[:, :nkv].reshape(n_p, t_p, fox_kvh, fox_hd))
            out["fv_p"].append(kv[:, nkv:].reshape(n_p, t_p, fox_kvh, fox_hd))
            out["fl_p"].append(lf.reshape(n_p, t_p, fox_heads))
            h2 = hs.reshape(n_s, dm)
            q = _mm(h2, fox_w_in_b, layer=j, col0=0, ncols=nq)
            kv = _mm(h2, fox_w_in_b, layer=j, col0=nq, ncols=2 * nkv)
            lf = _mm_logsig(h2, w_f, fox_b_f[j])
            o = _fox_sample_attention(q, kv[:, :nkv], kv[:, nkv:], lf, cache_fox_k, cache_fox_v,
                                      cache_fox_logf, j, page_table)
            os_ = _mm(o, fox_w_o_b, layer=j).reshape(1, n_s, dm)
            out["fk_s"].append(kv[:, :nkv].reshape(n_s, 1, fox_kvh, fox_hd))
            out["fv_s"].append(kv[:, nkv:].reshape(n_s, 1, fox_kvh, fox_hd))
            out["fl_s"].append(lf.reshape(n_s, 1, fox_heads))
        else:
            nlat = mla_ql + mla_kl
            w_pe = mla_w_in_b[j][:, nlat:]
            w_pe_rot = _swap_halves(w_pe, mla_r)
            wq = mla_w_uq_b[j].reshape(mla_ql, mla_heads, mla_dn + mla_r)
            wq_n = wq[:, :, :mla_dn].reshape(mla_ql, mla_heads * mla_dn)
            wq_r = wq[:, :, mla_dn:].reshape(mla_ql, mla_heads * mla_r)
            wq_r_rot = _swap_halves(wq_r, mla_r)

            def project(h2, cos, sin):
                z = _mm(h2, mla_w_in_b, layer=j, col0=0, ncols=nlat)
                cq, ckv, ckvb = _mla_norm(z, mla_q_norm[j], mla_kv_norm[j])
                kpe = _mm_rope(h2, w_pe, w_pe_rot, cos[:, :mla_r], sin[:, :mla_r], (F32,))
                qn = _mm(cq, wq_n, (BF16,))
                qr = _mm_rope(cq, wq_r, wq_r_rot, cos, sin, (BF16,))
                return qn, qr, ckv, ckvb, kpe

            qn, qr, ckv, ckvb, kpe = project(hp.reshape(n_p * t_p, dm), cos_p, sin_p)
            kvb = _mm(ckvb, mla_w_ukv_b, (BF16,), layer=j)
            qr4 = qr.reshape(n_p, t_p, mla_heads, mla_r).transpose(0, 2, 1, 3)
            o = _mla_prompt_attention(qn.reshape(n_p, t_p, -1), qr4, kvb.reshape(n_p, t_p, -1),
                                      kpe.astype(BF16).reshape(n_p, t_p, mla_r), mla_heads, mla_dn, mla_r, mla_dv)
            op = _mm(o.reshape(n_p * t_p, -1), mla_w_o_b, layer=j).reshape(n_p, t_p, dm)
            out["ckv_p"].append(ckv.reshape(n_p, t_p, mla_kl))
            out["kpe_p"].append(kpe.reshape(n_p, t_p, mla_r))
            qn, qr, ckv, ckvb, kpe = project(hs.reshape(n_s, dm), cos_s, sin_s)
            q_lat = _head_mm(qn, mla_w_ukv_b, j, mla_heads, 0, True, BF16)
            a = _mla_sample_attention(q_lat, qr, ckv, kpe, cache_mla_ckv, cache_mla_kpe, j, page_table, mla_heads,
                                      (mla_dn + mla_r) ** -0.5)
            o = _head_mm(a, mla_w_ukv_b, j, mla_heads, 1, False, BF16)
            os_ = _mm(o, mla_w_o_b, layer=j).reshape(1, n_s, dm)
            out["ckv_s"].append(ckv.reshape(n_s, 1, mla_kl))
            out["kpe_s"].append(kpe.reshape(n_s, 1, mla_r))
        (shp, scp, gate2_p), (shs, scs, gate2_s) = mods(i, 1)
        yp, hp = _ln_mod(yp, op, gate_p, ln_g[i, 0], ln_b[i, 0], scp, shp, alpha=alpha)
        ys, hs = _ln_mod(ys, os_, gate_s, ln_g[i, 0], ln_b[i, 0], scs, shs, alpha=alpha)
        fp = _peer_ffn(hp.reshape(n_p * t_p, dm), peer_w_q_b, peer_subkeys[i], peer_u_b, peer_v_b, i)
        fs = _peer_ffn(hs.reshape(n_s, dm), peer_w_q_b, peer_subkeys[i], peer_u_b, peer_v_b, i)
        fp, fs = fp.reshape(n_p, t_p, dm), fs.reshape(1, n_s, dm)
        if i + 1 < depth:
            (shp, scp, _), (shs, scs, _) = mods(i + 1, 0)
            yp, hp = _ln_mod(yp, fp, gate2_p, ln_g[i, 1], ln_b[i, 1], scp, shp, alpha=alpha)
            ys, hs = _ln_mod(ys, fs, gate2_s, ln_g[i, 1], ln_b[i, 1], scs, shs, alpha=alpha)
        else:
            yp = _ln_mod(yp, fp, gate2_p, ln_g[i, 1], ln_b[i, 1], alpha=alpha)
            ys = _ln_mod(ys, fs, gate2_s, ln_g[i, 1], ln_b[i, 1], alpha=alpha)

    st = lambda k: jnp.stack(out[k])
    return (yp, ys.reshape(n_s, 1, dm),
            st("s5r_p"), st("s5i_p"), st("fk_p"), st("fv_p"), st("fl_p"), st("ckv_p"), st("kpe_p"),
            st("s5r_s"), st("s5i_s"), st("fk_s"), st("fv_s"), st("fl_s"), st("ckv_s"), st("kpe_s"))
```

```python
import functools
import math

import jax
import jax.numpy as jnp
from jax import lax
from jax.experimental import pallas as pl
from jax.experimental.pallas import tpu as pltpu

F32 = jnp.float32
BF16 = jnp.bfloat16
HI = lax.Precision.HIGHEST

LANE = 128
SUBLANE = 8
VMEM_LIMIT = 56 * 1024 * 1024
LN_EPS = 1e-5
RMS_EPS = 1e-6
ROPE_BASE = 10000.0
LOG2E = math.log2(math.e)
S5_CHUNK = 8
PEER_TOPK = 16
NT = (((1,), (1,)), ((), ()))


def _cp(*sem):
    return pltpu.CompilerParams(dimension_semantics=sem, vmem_limit_bytes=VMEM_LIMIT)


def _pick(n, cands):
    for c in cands:
        if c <= n and n % c == 0:
            return c
    return n


def _wspec(w, k, tn, layer, col0, ngrid):
    off = col0 // tn
    assert col0 % tn == 0
    if w.ndim == 3:
        if ngrid == 1:
            return pl.BlockSpec((None, k, tn), lambda j: (layer, 0, off + j))
        return pl.BlockSpec((None, k, tn), lambda i, j: (layer, 0, off + j))
    if ngrid == 1:
        return pl.BlockSpec((k, tn), lambda j: (0, off + j))
    return pl.BlockSpec((k, tn), lambda i, j: (0, off + j))


def _mm_kernel(x_ref, w_ref, *o_refs):
    acc = jnp.dot(x_ref[...], w_ref[...], preferred_element_type=F32)
    for o_ref in o_refs:
        o_ref[...] = acc.astype(o_ref.dtype)


def _mm(x, w, out_dtypes=(F32,), layer=0, col0=0, ncols=None):
    m, k = x.shape
    n = w.shape[-1] - col0 if ncols is None else ncols
    tm = _pick(m, (1024, 512, 256, 128))
    tn = _pick(math.gcd(n, col0) if col0 else n, (1024, 512, 256, 128))
    outs = pl.pallas_call(
        _mm_kernel,
        grid=(m // tm, n // tn),
        in_specs=[pl.BlockSpec((tm, k), lambda i, j: (i, 0)), _wspec(w, k, tn, layer, col0, 2)],
        out_specs=[pl.BlockSpec((tm, tn), lambda i, j: (i, j)) for _ in out_dtypes],
        out_shape=[jax.ShapeDtypeStruct((m, n), d) for d in out_dtypes],
        compiler_params=_cp("parallel", "arbitrary"),
        name="mm",
    )(x, w)
    return outs[0] if len(out_dtypes) == 1 else outs


def _glu_kernel(x_ref, wv_ref, wg_ref, o_ref):
    x = x_ref[...]
    val = jnp.dot(x, wv_ref[...], preferred_element_type=F32)
    gate = jnp.dot(x, wg_ref[...], preferred_element_type=F32)
    o_ref[...] = val * jax.nn.sigmoid(gate)


def _mm_glu(x, w, layer):
    m, k = x.shape
    n = w.shape[-1] // 2
    tm = _pick(m, (1024, 512, 256, 128))
    tn = _pick(n, (512, 256, 128))
    return pl.pallas_call(
        _glu_kernel,
        grid=(m // tm, n // tn),
        in_specs=[pl.BlockSpec((tm, k), lambda i, j: (i, 0)),
                  _wspec(w, k, tn, layer, 0, 2), _wspec(w, k, tn, layer, n, 2)],
        out_specs=pl.BlockSpec((tm, tn), lambda i, j: (i, j)),
        out_shape=jax.ShapeDtypeStruct((m, n), F32),
        compiler_params=_cp("parallel", "arbitrary"),
        name="mm_glu",
    )(x, w, w)


def _logsig_kernel(x_ref, w_ref, b_ref, o_ref):
    z = jnp.dot(x_ref[...], w_ref[...], preferred_element_type=F32) + b_ref[...]
    o_ref[...] = jnp.minimum(z, 0.0) - jnp.log1p(jnp.exp(-jnp.abs(z)))


def _mm_logsig(x, w, b):
    m, k = x.shape
    n = w.shape[1]
    tm = _pick(m, (1024, 512, 256, 128))
    return pl.pallas_call(
        _logsig_kernel,
        grid=(m // tm,),
        in_specs=[pl.BlockSpec((tm, k), lambda i: (i, 0)),
                  pl.BlockSpec((k, n), lambda i: (0, 0)),
                  pl.BlockSpec((1, n), lambda i: (0, 0))],
        out_specs=pl.BlockSpec((tm, n), lambda i: (i, 0)),
        out_shape=jax.ShapeDtypeStruct((m, n), F32),
        compiler_params=_cp("parallel"),
        name="mm_logsig",
    )(x, w, b.reshape(1, n).astype(F32))


def _rope_kernel(x_ref, w_ref, wr_ref, cos_ref, sin_ref, *o_refs):
    x = x_ref[...]
    a = jnp.dot(x, w_ref[...], preferred_element_type=F32)
    b = jnp.dot(x, wr_ref[...], preferred_element_type=F32)
    reps = a.shape[1] // cos_ref.shape[1]
    cos, sin = cos_ref[...], sin_ref[...]
    if reps > 1:
        cos = jnp.concatenate([cos] * reps, axis=1)
        sin = jnp.concatenate([sin] * reps, axis=1)
    out = a * cos + b * sin
    for o_ref in o_refs:
        o_ref[...] = out.astype(o_ref.dtype)


def _mm_rope(x, w, w_rot, cos, sin, out_dtypes):
    m, k = x.shape
    n = w.shape[1]
    period = cos.shape[1]
    tm = _pick(m, (1024, 512, 256, 128))
    tn = _pick(n, (1024, 512, 256, 128)) if n % period == 0 and period % LANE == 0 else n
    outs = pl.pallas_call(
        _rope_kernel,
        grid=(m // tm, n // tn),
        in_specs=[pl.BlockSpec((tm, k), lambda i, j: (i, 0)),
                  pl.BlockSpec((k, tn), lambda i, j: (0, j)),
                  pl.BlockSpec((k, tn), lambda i, j: (0, j)),
                  pl.BlockSpec((tm, period), lambda i, j: (i, 0)),
                  pl.BlockSpec((tm, period), lambda i, j: (i, 0))],
        out_specs=[pl.BlockSpec((tm, tn), lambda i, j: (i, j)) for _ in out_dtypes],
        out_shape=[jax.ShapeDtypeStruct((m, n), d) for d in out_dtypes],
        compiler_params=_cp("parallel", "arbitrary"),
        name="mm_rope",
    )(x, w, w_rot, cos, sin)
    return outs[0] if len(out_dtypes) == 1 else outs


def _headmm_kernel(x_ref, w_ref, o_ref, *, trans_w):
    if trans_w:
        acc = lax.dot_general(x_ref[...], w_ref[...], NT, preferred_element_type=F32)
    else:
        acc = jnp.dot(x_ref[...], w_ref[...], preferred_element_type=F32)
    o_ref[...] = acc.astype(o_ref.dtype)


def _head_mm(x, w, layer, heads, w_off, trans_w, out_dtype):
    m = x.shape[0]
    dx = x.shape[1] // heads
    r = w.shape[1]
    dw = w.shape[2] // (2 * heads)
    do = r if trans_w else dw
    return pl.pallas_call(
        functools.partial(_headmm_kernel, trans_w=trans_w),
        grid=(heads,),
        in_specs=[pl.BlockSpec((m, dx), lambda h: (0, h)),
                  pl.BlockSpec((None, r, dw), lambda h: (layer, 0, 2 * h + w_off))],
        out_specs=pl.BlockSpec((m, do), lambda h: (0, h)),
        out_shape=jax.ShapeDtypeStruct((m, heads * do), out_dtype),
        compiler_params=_cp("parallel"),
        name="head_mm",
    )(x, w)


def _mod_kernel(c_ref, w_ref, b_ref, o_ref):
    c = c_ref[...]
    a = (c * jax.nn.sigmoid(c)).astype(BF16)
    o_ref[0] = jnp.dot(a, w_ref[0].astype(BF16), preferred_element_type=F32) + b_ref[0]


def _modulation_all(c_all, ada_w, ada_b):
    r, d = c_all.shape
    depth = ada_w.shape[0]
    s = depth * 2
    n3 = ada_w.shape[-1]
    w = ada_w.reshape(s, d, n3)
    b = ada_b.reshape(s, 1, n3)
    tn = _pick(n3, (512, 256, 128))
    return pl.pallas_call(
        _mod_kernel,
        grid=(s, n3 // tn),
        in_specs=[pl.BlockSpec((r, d), lambda i, j: (0, 0)),
                  pl.BlockSpec((1, d, tn), lambda i, j: (i, 0, j)),
                  pl.BlockSpec((1, 1, tn), lambda i, j: (i, 0, j))],
        out_specs=pl.BlockSpec((1, r, tn), lambda i, j: (i, 0, j)),
        out_shape=jax.ShapeDtypeStruct((s, r, n3), F32),
        compiler_params=_cp("parallel", "arbitrary"),
        name="modulation",
    )(c_all, w, b)


def _ln_mod_kernel(*refs, alpha, do_ln, do_mod):
    it = iter(refs)
    x_ref = next(it)
    if do_ln:
        f_ref, gate_ref, g_ref, b_ref = next(it), next(it), next(it), next(it)
    if do_mod:
        scale_ref, shift_ref = next(it), next(it)
    if do_ln:
        y_ref = next(it)
    if do_mod:
        h_ref = next(it)
    y = x_ref[0]
    if do_ln:
        xf = alpha * y + (1.0 + gate_ref[0]) * f_ref[0]
        mu = jnp.mean(xf, axis=-1, keepdims=True)
        xc = xf - mu
        var = jnp.mean(xc * xc, axis=-1, keepdims=True)
        y = xc * lax.rsqrt(var + LN_EPS) * g_ref[...] + b_ref[...]
        y_ref[0] = y
    if do_mod:
        h_ref[0] = (y * (1.0 + scale_ref[0]) + shift_ref[0]).astype(BF16)


def _ln_mod(x, f=None, gate=None, g=None, b=None, scale=None, shift=None, *, alpha):
    n, t, d = x.shape
    do_ln, do_mod = f is not None, scale is not None
    tt = _pick(t, (256, 128))
    xspec = pl.BlockSpec((1, tt, d), lambda i, j: (i, j, 0))

    def mspec(a):
        if a.shape[1] == 1:
            return pl.BlockSpec((1, 1, d), lambda i, j: (i, 0, 0))
        return xspec

    vspec = pl.BlockSpec((1, d), lambda i, j: (0, 0))
    args, specs, oshapes, ospecs = [x], [xspec], [], []
    if do_ln:
        args += [f, gate, g.reshape(1, d), b.reshape(1, d)]
        specs += [xspec, mspec(gate), vspec, vspec]
        oshapes.append(jax.ShapeDtypeStruct((n, t, d), F32))
        ospecs.append(xspec)
    if do_mod:
        args += [scale, shift]
        specs += [mspec(scale), mspec(shift)]
        oshapes.append(jax.ShapeDtypeStruct((n, t, d), BF16))
        ospecs.append(xspec)
    outs = pl.pallas_call(
        functools.partial(_ln_mod_kernel, alpha=alpha, do_ln=do_ln, do_mod=do_mod),
        grid=(n, t // tt),
        in_specs=specs, out_specs=ospecs, out_shape=oshapes,
        compiler_params=_cp("parallel", "parallel"),
        name="ln_mod",
    )(*args)
    return outs if len(outs) > 1 else outs[0]


def _s5_discretize(lam_re, lam_im, log_step, b_re, b_im):
    lr, li = lam_re.astype(F32), lam_im.astype(F32)
    dt = jnp.exp(log_step.astype(F32))[:, None]
    mag, ang = jnp.exp(lr * dt), li * dt
    a_re, a_im = mag * jnp.cos(ang), mag * jnp.sin(ang)
    den = lr * lr + li * li
    f_re = ((a_re - 1.0) * lr + a_im * li) / den
    f_im = (a_im * lr - (a_re - 1.0) * li) / den
    br, bi = b_re.astype(F32), b_im.astype(F32)
    bb_re = f_re[..., None] * br - f_im[..., None] * bi
    bb_im = f_re[..., None] * bi + f_im[..., None] * br
    return lr * dt, li * dt, a_re, a_im, bb_re, bb_im


def _s5_powers(ldt_re, ldt_im, ks):
    k = jnp.asarray(ks, F32)[:, None, None]
    mag, ang = jnp.exp(ldt_re * k), ldt_im * k
    return mag * jnp.cos(ang), mag * jnp.sin(ang)


def _block_diag(x, gl):
    *lead, g, a, b = x.shape
    q = g // gl
    xq = x.reshape(*lead, q, gl, a, b)
    eye = jnp.eye(gl, dtype=x.dtype)
    out = xq[..., :, :, :, None, :] * eye[:, None, :, None]
    return out.reshape(*lead, q, gl * a, gl * b)


def _s5_operators(prm, chunk, n_chunks):
    lam_re, lam_im, log_step, b_re, b_im, c_re, c_im = prm
    g, p = lam_re.shape
    c = b_re.shape[-1]
    gl = LANE // c
    q = g // gl
    ldr, ldi, _, _, bb_re, bb_im = _s5_discretize(lam_re, lam_im, log_step, b_re, b_im)
    cr, ci = c_re.astype(F32), c_im.astype(F32)
    ak_re, ak_im = _s5_powers(ldr, ldi, list(range(chunk + 1)))
    m_re = ak_re[..., None] * bb_re - ak_im[..., None] * bb_im
    m_im = ak_re[..., None] * bb_im + ak_im[..., None] * bb_re
    kk = (jnp.einsum("gop,kgpi->kgio", cr, m_re[:chunk], precision=HI)
          - jnp.einsum("gop,kgpi->kgio", ci, m_im[:chunk], precision=HI))
    kc = kk.reshape(chunk, q, gl * c, c).transpose(1, 0, 2, 3)
    rev = jnp.arange(chunk - 1, -1, -1)
    mc = jnp.stack([m_re[rev], m_im[rev]]).transpose(0, 1, 2, 4, 3)
    mc = mc.reshape(2, chunk, q, gl * c, p).transpose(2, 0, 1, 3, 4)
    pr, pi = ak_re[1:], ak_im[1:]
    v_re = cr[None] * pr[:, :, None, :] - ci[None] * pi[:, :, None, :]
    v_im = -cr[None] * pi[:, :, None, :] - ci[None] * pr[:, :, None, :]
    vc = jnp.stack([v_re, v_im]).transpose(0, 1, 2, 4, 3)
    vc = vc.reshape(2, chunk, q, gl * p, c).transpose(2, 0, 1, 3, 4)
    del n_chunks
    exps = [chunk * (i + 1) for i in range(SUBLANE)] + [chunk * (1 << j) for j in range(_log2(SUBLANE))]
    ad_re, ad_im = _s5_powers(ldr, ldi, exps)
    ad = jnp.concatenate([ad_re.reshape(len(exps), q, gl * p), ad_im.reshape(len(exps), q, gl * p)], axis=-1)
    rows = -(-len(exps) // SUBLANE) * SUBLANE
    ad = jnp.pad(ad.transpose(1, 0, 2), ((0, 0), (0, rows - len(exps)), (0, 0)))
    return kc, mc, vc, ad


def _s5_step_operators(prm):
    lam_re, lam_im, log_step, b_re, b_im, c_re, c_im = prm
    g, p = lam_re.shape
    gl = LANE // b_re.shape[-1]
    q = g // gl
    _, _, a_re, a_im, bb_re, bb_im = _s5_discretize(lam_re, lam_im, log_step, b_re, b_im)
    bb = jnp.concatenate([_block_diag(bb_re.transpose(0, 2, 1), gl),
                          _block_diag(bb_im.transpose(0, 2, 1), gl)], axis=-1).astype(BF16)
    cr, ci = c_re.astype(F32), c_im.astype(F32)
    cc = jnp.concatenate([_block_diag(cr.transpose(0, 2, 1), gl),
                          _block_diag(-ci.transpose(0, 2, 1), gl)], axis=1).astype(BF16)
    a = jnp.concatenate([a_re.reshape(q, 1, gl * p), a_im.reshape(q, 1, gl * p)], axis=-1)
    return bb, cc, a


def _log2(x):
    assert x & (x - 1) == 0, "power of two expected"
    return x.bit_length() - 1


def _lane_tiler(src, dst):
    r = lax.broadcasted_iota(jnp.int32, (src, dst), 0)
    c = lax.broadcasted_iota(jnp.int32, (src, dst), 1)
    return ((c & (src - 1)) == r).astype(BF16)


def _same_block(rows, rblk, cols, cblk):
    r = lax.shift_right_logical(lax.broadcasted_iota(jnp.int32, (rows, cols), 0), _log2(rblk))
    c = lax.shift_right_logical(lax.broadcasted_iota(jnp.int32, (rows, cols), 1), _log2(cblk))
    return r == c


def _s5_scan_kernel(u_ref, kc_ref, mc_ref, vc_ref, ad_ref, d_ref, y_ref, sfin_ref, t_scr, w_scr, v_scr, e_scr,
                    *, n, nk, chunk):
    c = kc_ref.shape[-1]
    p = mc_ref.shape[-1]
    sw = (LANE // c) * p
    tile_c, tile_p = _lane_tiler(c, LANE), _lane_tiler(p, sw)
    mask_t = _same_block(LANE, c, LANE, c)
    mask_w = _same_block(LANE, c, sw, p)
    mask_v = _same_block(sw, p, LANE, c)

    def expand(x, tiler, mask):
        return jnp.where(mask, jnp.dot(x.astype(BF16), tiler, preferred_element_type=F32), 0.0).astype(BF16)

    bd = [expand(kc_ref[0, k], tile_c, mask_t) for k in range(chunk)]
    zero_blk = jnp.zeros((LANE, LANE), BF16)
    for ti in range(chunk):
        rows = slice(ti * LANE, (ti + 1) * LANE)
        for to in range(chunk):
            t_scr[rows, to * LANE:(to + 1) * LANE] = bd[to - ti] if to >= ti else zero_blk
        for ri in range(2):
            w_scr[rows, ri * sw:(ri + 1) * sw] = expand(mc_ref[0, ri, ti], tile_p, mask_w)
    for ri in range(2):
        for to in range(chunk):
            v_scr[ri * sw:(ri + 1) * sw, to * LANE:(to + 1) * LANE] = expand(vc_ref[0, ri, to], tile_c, mask_v)

    u = u_ref[0]
    ub = u.astype(BF16)
    x = jnp.dot(ub, w_scr[...], preferred_element_type=F32)
    half = x.shape[1] // 2
    xr, xi = x[:, :half], x[:, half:]
    nrows = x.shape[0]
    k = lax.rem(lax.broadcasted_iota(jnp.int32, (nrows, 1), 0), nk)
    kin = k & (SUBLANE - 1)
    for j in range(_log2(SUBLANE)):
        d = 1 << j
        ar, ai = ad_ref[0, SUBLANE + j:SUBLANE + j + 1, :half], ad_ref[0, SUBLANE + j:SUBLANE + j + 1, half:]
        keep = kin >= d
        sr = jnp.where(keep, pltpu.roll(xr, d, 0), 0.0)
        si = jnp.where(keep, pltpu.roll(xi, d, 0), 0.0)
        xr, xi = xr + (ar * sr - ai * si), xi + (ar * si + ai * sr)
    e_scr[:, :half] = xr
    e_scr[:, half:] = xi
    pr, pi = ad_ref[0, 0:SUBLANE, :half], ad_ref[0, 0:SUBLANE, half:]
    tiles = nk // SUBLANE
    carries = [None] * n
    for j in range(tiles):
        for s in range(n):
            rows = slice((s * tiles + j) * SUBLANE, (s * tiles + j + 1) * SUBLANE)
            tr, ti = e_scr[rows, :half], e_scr[rows, half:]
            if j > 0:
                cr, ci = carries[s]
                tr, ti = tr + (pr * cr - pi * ci), ti + (pr * ci + pi * cr)
                e_scr[rows, :half] = tr
                e_scr[rows, half:] = ti
            carries[s] = (tr[SUBLANE - 1:SUBLANE], ti[SUBLANE - 1:SUBLANE])
    xr, xi = e_scr[:, :half], e_scr[:, half:]
    keep = k >= 1
    s0 = jnp.concatenate([jnp.where(keep, pltpu.roll(xr, 1, 0), 0.0),
                          jnp.where(keep, pltpu.roll(xi, 1, 0), 0.0)], axis=1).astype(BF16)
    y = jnp.dot(ub, t_scr[...], preferred_element_type=F32)
    y = y + jnp.dot(s0, v_scr[...], preferred_element_type=F32)
    y = y + d_ref[0] * u
    y_ref[0] = jax.nn.gelu(y).astype(BF16)
    for s in range(n):
        last = (s + 1) * nk - 1
        sfin_ref[0, s:s + 1, :] = jnp.concatenate([xr[last:last + 1], xi[last:last + 1]], axis=1)


def _s5_prompt(h, prm, w_in, d_skip, w_glu, layer):
    n, t, dm = h.shape
    g, p = prm[0].shape
    c = prm[3].shape[-1]
    gl = LANE // c
    q = g // gl
    chunk = S5_CHUNK
    nk = t // chunk
    u = _mm(h.reshape(n * t, dm), w_in, layer=layer)
    uc = u.reshape(n, nk, chunk, q, LANE).transpose(3, 0, 1, 2, 4).reshape(q, n * nk, chunk * LANE)
    assert nk % SUBLANE == 0, "the chunk scan works on whole 8-chunk tiles"
    kc, mc, vc, ad = _s5_operators(prm, chunk, nk)
    d_t = jnp.tile(d_skip.astype(F32).reshape(q, 1, LANE), (1, 1, chunk))
    width = chunk * LANE
    sw = 2 * gl * p
    yc, sfin = pl.pallas_call(
        functools.partial(_s5_scan_kernel, n=n, nk=nk, chunk=chunk),
        grid=(q,),
        in_specs=[pl.BlockSpec((1, n * nk, width), lambda i: (i, 0, 0)),
                  pl.BlockSpec((1, chunk, LANE, c), lambda i: (i, 0, 0, 0)),
                  pl.BlockSpec((1, 2, chunk, LANE, p), lambda i: (i, 0, 0, 0, 0)),
                  pl.BlockSpec((1, 2, chunk, gl * p, c), lambda i: (i, 0, 0, 0, 0)),
                  pl.BlockSpec((1, ad.shape[1], sw), lambda i: (i, 0, 0)),
                  pl.BlockSpec((1, 1, width), lambda i: (i, 0, 0))],
        out_specs=[pl.BlockSpec((1, n * nk, width), lambda i: (i, 0, 0)),
                   pl.BlockSpec((1, n, sw), lambda i: (i, 0, 0))],
        out_shape=[jax.ShapeDtypeStruct((q, n * nk, width), BF16),
                   jax.ShapeDtypeStruct((q, n, sw), F32)],
        scratch_shapes=[pltpu.VMEM((width, width), BF16), pltpu.VMEM((width, sw), BF16),
                        pltpu.VMEM((sw, width), BF16), pltpu.VMEM((n * nk, sw), F32)],
        compiler_params=_cp("parallel"),
        name="s5_scan",
    )(uc, kc, mc, vc, ad, d_t)
    y = yc.reshape(q, n, nk, chunk, LANE).transpose(1, 2, 3, 0, 4).reshape(n * t, dm)
    out = _mm_glu(y, w_glu, layer).reshape(n, t, dm)
    sf = sfin.reshape(q, n, 2, gl, p).transpose(2, 1, 0, 3, 4).reshape(2, n, g, p)
    return out, sf[0], sf[1]


def _s5_step_kernel(u_ref, sr_ref, si_ref, bb_ref, cc_ref, a_ref, d_ref, y_ref, or_ref, oi_ref):
    u = u_ref[...]
    bu = jnp.dot(u.astype(BF16), bb_ref[0], preferred_element_type=F32)
    half = bu.shape[1] // 2
    ar, ai = a_ref[0, :, :half], a_ref[0, :, half:]
    sr, si = sr_ref[...], si_ref[...]
    nr = bu[:, :half] + (ar * sr - ai * si)
    ni = bu[:, half:] + (ar * si + ai * sr)
    or_ref[...] = nr
    oi_ref[...] = ni
    s = jnp.concatenate([nr, ni], axis=1).astype(BF16)
    y = jnp.dot(s, cc_ref[0], preferred_element_type=F32) + d_ref[0] * u
    y_ref[...] = jax.nn.gelu(y).astype(BF16)


def _s5_sample(h, s0_re, s0_im, prm, w_in, d_skip, w_glu, layer):
    n, _, dm = h.shape
    g, p = prm[0].shape
    gl = LANE // prm[3].shape[-1]
    q = g // gl
    sw = gl * p
    u = _mm(h.reshape(n, dm), w_in, layer=layer)
    bb, cc, a = _s5_step_operators(prm)
    y, nr, ni = pl.pallas_call(
        _s5_step_kernel,
        grid=(q,),
        in_specs=[pl.BlockSpec((n, LANE), lambda i: (0, i)),
                  pl.BlockSpec((n, sw), lambda i: (0, i)),
                  pl.BlockSpec((n, sw), lambda i: (0, i)),
                  pl.BlockSpec((1, LANE, 2 * sw), lambda i: (i, 0, 0)),
                  pl.BlockSpec((1, 2 * sw, LANE), lambda i: (i, 0, 0)),
                  pl.BlockSpec((1, 1, 2 * sw), lambda i: (i, 0, 0)),
                  pl.BlockSpec((1, 1, LANE), lambda i: (i, 0, 0))],
        out_specs=[pl.BlockSpec((n, LANE), lambda i: (0, i)),
                   pl.BlockSpec((n, sw), lambda i: (0, i)),
                   pl.BlockSpec((n, sw), lambda i: (0, i))],
        out_shape=[jax.ShapeDtypeStruct((n, dm), BF16),
                   jax.ShapeDtypeStruct((n, g * p), F32),
                   jax.ShapeDtypeStruct((n, g * p), F32)],
        compiler_params=_cp("parallel"),
        name="s5_step",
    )(u, s0_re.astype(F32).reshape(n, g * p), s0_im.astype(F32).reshape(n, g * p), bb, cc, a,
      d_skip.astype(F32).reshape(q, 1, LANE))
    out = _mm_glu(y, w_glu, layer).reshape(n, 1, dm)
    return out, nr.reshape(n, g, p), ni.reshape(n, g, p)


def _tri_lower(nrows, ncols, strict):
    r = lax.broadcasted_iota(jnp.int32, (nrows, ncols), 0)
    c = lax.broadcasted_iota(jnp.int32, (nrows, ncols), 1)
    return ((c < r) if strict else (c <= r)).astype(F32)


def _cumsum_kernel(x_ref, o_ref, *, blk):
    t = x_ref.shape[1]
    tri = _tri_lower(blk, blk, strict=False)
    carry = jnp.zeros((1, x_ref.shape[2]), F32)
    for i in range(t // blk):
        xb = x_ref[0, i * blk:(i + 1) * blk, :]
        cb = jnp.dot(tri, xb, preferred_element_type=F32, precision=HI) + carry
        o_ref[0, i * blk:(i + 1) * blk, :] = cb
        carry = cb[blk - 1:blk, :]


def _cumsum_time(x):
    n, t, hh = x.shape
    blk = _pick(t, (128,))
    return pl.pallas_call(
        functools.partial(_cumsum_kernel, blk=blk),
        grid=(n,),
        in_specs=[pl.BlockSpec((1, t, hh), lambda i: (i, 0, 0))],
        out_specs=pl.BlockSpec((1, t, hh), lambda i: (i, 0, 0)),
        out_shape=jax.ShapeDtypeStruct((n, t, hh), F32),
        compiler_params=_cp("parallel"),
        name="cumsum_time",
    )(x)


def _softmax_update(s, m_prev, l_prev):
    m_new = jnp.maximum(m_prev, jnp.max(s, axis=-1, keepdims=True))
    corr = jnp.exp(m_prev - m_new)
    p = jnp.exp(s - m_new)
    return m_new, corr, p, l_prev * corr + jnp.sum(p, axis=-1, keepdims=True)


def _softmax_update2(s, m_prev, l_prev):
    m_new = jnp.maximum(m_prev, jnp.max(s, axis=-1, keepdims=True))
    corr = jnp.exp2(m_prev - m_new)
    p = jnp.exp2(s - m_new)
    return m_new, corr, p, l_prev * corr + jnp.sum(p, axis=-1, keepdims=True)


def _causal_blocks(qi, kv_step, init):
    carry = lax.fori_loop(0, qi, lambda j, c: kv_step(j, c, False), init)
    return kv_step(qi, carry, True)


def _fox_attn_kernel(q_ref, k_ref, v_ref, fq_ref, fk_ref, o_ref, *, gq, tq, scale):
    g = lax.rem(pl.program_id(1), gq)
    qi = pl.program_id(2)
    q = q_ref[0]
    lane = lax.broadcasted_iota(jnp.int32, fq_ref.shape[2:], 1)
    fq = jnp.sum(jnp.where(lane == g, fq_ref[0, 0], 0.0), axis=1, keepdims=True)
    rowpos = qi * tq + lax.broadcasted_iota(jnp.int32, (tq, 1), 0)

    def kv_step(j, carry, diagonal):
        m_prev, l_prev, acc = carry
        start = pl.multiple_of(j * tq, tq)
        s = lax.dot_general(q, k_ref[0, pl.ds(start, tq), :], NT, preferred_element_type=F32) * scale2
        s = s + (fq2 - fk_ref[0, 0, j, pl.ds(g, 1), :] * LOG2E)
        if diagonal:
            s = jnp.where(rowpos >= (j * tq + lax.broadcasted_iota(jnp.int32, (1, tq), 1)), s, -jnp.inf)
        m_new, corr, p, l_new = _softmax_update2(s, m_prev, l_prev)
        acc = acc * corr + jnp.dot(p.astype(BF16), v_ref[0, pl.ds(start, tq), :], preferred_element_type=F32)
        return m_new, l_new, acc

    scale2 = scale * LOG2E
    fq2 = fq * LOG2E
    init = (jnp.full((tq, 1), -jnp.inf, F32), jnp.zeros((tq, 1), F32), jnp.zeros((tq, v_ref.shape[2]), F32))
    _, l, acc = _causal_blocks(qi, kv_step, init)
    o_ref[0] = (acc / l).astype(o_ref.dtype)


def _fox_prompt_attention(q, kvb, f_cum, kvh, hd):
    n, t, qd = q.shape
    heads = qd // hd
    gq = heads // kvh
    tq = _pick(t, (512, 256, 128))
    nq = t // tq
    fq = f_cum.reshape(n, t, kvh, gq).transpose(0, 2, 1, 3)
    fk = f_cum.reshape(n, nq, tq, kvh, gq).transpose(0, 3, 1, 4, 2)
    return pl.pallas_call(
        functools.partial(_fox_attn_kernel, gq=gq, tq=tq, scale=hd ** -0.5),
        grid=(n, heads, nq),
        in_specs=[pl.BlockSpec((1, tq, hd), lambda b, h, i: (b, i, h)),
                  pl.BlockSpec((1, t, hd), lambda b, h, i: (b, 0, h // gq)),
                  pl.BlockSpec((1, t, hd), lambda b, h, i: (b, 0, kvh + h // gq)),
                  pl.BlockSpec((1, 1, tq, gq), lambda b, h, i: (b, h // gq, i, 0)),
                  pl.BlockSpec((1, 1, nq, gq, tq), lambda b, h, i: (b, h // gq, 0, 0, 0))],
        out_specs=pl.BlockSpec((1, tq, hd), lambda b, h, i: (b, i, h)),
        out_shape=jax.ShapeDtypeStruct((n, t, qd), BF16),
        compiler_params=_cp("parallel", "parallel", "arbitrary"),
        name="fox_attn",
    )(q, kvb, kvb, fq, fk)


def _mla_attn_kernel(qn_ref, qr_ref, kn_ref, kr_ref, v_ref, o_ref, *, tq, scale):
    qi = pl.program_id(2)
    qn = qn_ref[0]
    qr = qr_ref[0, 0]
    rowpos = qi * tq + lax.broadcasted_iota(jnp.int32, (tq, 1), 0)

    def kv_step(j, carry, diagonal):
        m_prev, l_prev, acc = carry
        start = pl.multiple_of(j * tq, tq)
        s = lax.dot_general(qn, kn_ref[0, pl.ds(start, tq), :], NT, preferred_element_type=F32)
        s = (s + lax.dot_general(qr, kr_ref[0, pl.ds(start, tq), :], NT, preferred_element_type=F32)) * scale2
        if diagonal:
            s = jnp.where(rowpos >= (j * tq + lax.broadcasted_iota(jnp.int32, (1, tq), 1)), s, -jnp.inf)
        m_new, corr, p, l_new = _softmax_update2(s, m_prev, l_prev)
        acc = acc * corr + jnp.dot(p.astype(BF16), v_ref[0, pl.ds(start, tq), :], preferred_element_type=F32)
        return m_new, l_new, acc

    scale2 = scale * LOG2E
    init = (jnp.full((tq, 1), -jnp.inf, F32), jnp.zeros((tq, 1), F32), jnp.zeros((tq, v_ref.shape[2]), F32))
    _, l, acc = _causal_blocks(qi, kv_step, init)
    o_ref[0] = (acc / l).astype(o_ref.dtype)


def _mla_prompt_attention(qn, qr, kvb, kr, heads, dn, dr, dv):
    n, t, _ = qn.shape
    tq = _pick(t, (512, 256, 128))
    nq = t // tq
    return pl.pallas_call(
        functools.partial(_mla_attn_kernel, tq=tq, scale=(dn + dr) ** -0.5),
        grid=(n, heads, nq),
        in_specs=[pl.BlockSpec((1, tq, dn), lambda b, h, i: (b, i, h)),
                  pl.BlockSpec((1, 1, tq, dr), lambda b, h, i: (b, h, i, 0)),
                  pl.BlockSpec((1, t, dn), lambda b, h, i: (b, 0, 2 * h)),
                  pl.BlockSpec((1, t, dr), lambda b, h, i: (b, 0, 0)),
                  pl.BlockSpec((1, t, dv), lambda b, h, i: (b, 0, 2 * h + 1))],
        out_specs=pl.BlockSpec((1, tq, dv), lambda b, h, i: (b, i, h)),
        out_shape=jax.ShapeDtypeStruct((n, t, heads * dv), BF16),
        compiler_params=_cp("parallel", "parallel", "arbitrary"),
        name="mla_attn",
    )(qn, qr, kvb, kr, kvb)


def _fox_decode_kernel(pt_ref, *refs, pp, kvh, gq, hd, scale):
    del pt_ref
    q_ref, knew_ref, vnew_ref, cnew_ref = refs[:4]
    k_refs = refs[4:4 + pp]
    v_refs = refs[4 + pp:4 + 2 * pp]
    f_refs = refs[4 + 2 * pp:4 + 3 * pp]
    o_ref = refs[4 + 3 * pp]
    m_ref, l_ref, acc_ref, carry_ref = refs[5 + 3 * pp:]
    step = pl.program_id(1)
    ps = f_refs[0].shape[2]

    @pl.when(step == 0)
    def _():
        m_ref[...] = jnp.full(m_ref.shape, -jnp.inf, F32)
        l_ref[...] = jnp.zeros(l_ref.shape, F32)
        acc_ref[...] = jnp.zeros(acc_ref.shape, F32)
        carry_ref[...] = jnp.zeros(carry_ref.shape, F32)

    q = q_ref[0]
    cnew = cnew_ref[0]
    later = (lax.broadcasted_iota(jnp.int32, (ps, ps), 0) > lax.broadcasted_iota(jnp.int32, (ps, ps), 1)).astype(F32)
    carry = carry_ref[...]
    hh = kvh * gq
    own = (lax.shift_right_logical(lax.broadcasted_iota(jnp.int32, (hh, kvh * ps), 0), _log2(gq))
           == lax.shift_right_logical(lax.broadcasted_iota(jnp.int32, (hh, kvh * ps), 1), _log2(ps)))

    def by_head(ref):
        return jnp.concatenate([ref[0, pl.ds(kk, ps, stride=kvh), :] for kk in range(kvh)], axis=0).astype(BF16)

    scores = []
    for i in range(pp):
        lf = f_refs[i][0]
        after = jnp.dot(lf, later, preferred_element_type=F32, precision=HI) + carry
        carry = carry + jnp.sum(lf, axis=1, keepdims=True)
        bias = cnew + after
        s = lax.dot_general(q, by_head(k_refs[i]), NT, preferred_element_type=F32) * scale
        scores.append(jnp.where(own, s + jnp.concatenate([bias] * kvh, axis=1), -jnp.inf))
    carry_ref[...] = carry
    s_all = jnp.concatenate(scores, axis=1)
    m_new, corr, p_all, l_new = _softmax_update(s_all, m_ref[...], l_ref[...])
    m_ref[...] = m_new
    l_ref[...] = l_new
    pb = p_all.astype(BF16)
    acc = acc_ref[...] * corr
    width = kvh * ps
    for i in range(pp):
        acc = acc + jnp.dot(pb[:, i * width:(i + 1) * width], by_head(v_refs[i]), preferred_element_type=F32)
    acc_ref[...] = acc

    @pl.when(step == pl.num_programs(1) - 1)
    def _():
        qf = q.astype(F32)
        kn = knew_ref[0].astype(BF16).astype(F32)
        vn = vnew_ref[0].astype(BF16).astype(F32)
        kx = jnp.concatenate([jnp.broadcast_to(kn[kk:kk + 1], (gq, hd)) for kk in range(kvh)], axis=0)
        vx = jnp.concatenate([jnp.broadcast_to(vn[kk:kk + 1], (gq, hd)) for kk in range(kvh)], axis=0)
        s_new = jnp.sum(qf * kx, axis=1, keepdims=True) * scale + (cnew - cnew)
        m_fin, corr_fin, p_new, l_fin = _softmax_update(s_new, m_ref[...], l_ref[...])
        acc = acc_ref[...] * corr_fin + p_new.astype(BF16).astype(F32) * vx
        o_ref[0] = acc / l_fin


def _fox_sample_attention(q, k_new, v_new, logf_new, cache_k, cache_v, cache_logf, layer, page_table):
    n = q.shape[0]
    _, pool, ps, kvh, hd = cache_k.shape
    hh = cache_logf.shape[-1]
    gq = hh // kvh
    npg = page_table.shape[1]
    pp = _pick(npg, (16, 8, 4, 2, 1))
    ck = cache_k.reshape(-1, ps * kvh, hd)
    cv = cache_v.reshape(-1, ps * kvh, hd)
    cf = cache_logf.reshape(-1, ps, hh).transpose(0, 2, 1)
    base = layer * pool

    def page_map(i):
        return lambda s, p, pt: (base + pt[s, npg - 1 - (p * pp + i)], 0, 0)

    seq3 = lambda s, p, pt: (s, 0, 0)
    in_specs = [pl.BlockSpec((1, hh, hd), seq3),
                pl.BlockSpec((1, kvh, hd), seq3),
                pl.BlockSpec((1, kvh, hd), seq3),
                pl.BlockSpec((1, hh, 1), seq3)]
    in_specs += [pl.BlockSpec((1, ps * kvh, hd), page_map(i)) for i in range(pp)]
    in_specs += [pl.BlockSpec((1, ps * kvh, hd), page_map(i)) for i in range(pp)]
    in_specs += [pl.BlockSpec((1, hh, ps), page_map(i)) for i in range(pp)]
    out = pl.pallas_call(
        functools.partial(_fox_decode_kernel, pp=pp, kvh=kvh, gq=gq, hd=hd, scale=hd ** -0.5),
        grid_spec=pltpu.PrefetchScalarGridSpec(
            num_scalar_prefetch=1,
            grid=(n, npg // pp),
            in_specs=in_specs,
            out_specs=pl.BlockSpec((1, hh, hd), seq3),
            scratch_shapes=[pltpu.VMEM((hh, 1), F32), pltpu.VMEM((hh, 1), F32),
                            pltpu.VMEM((hh, hd), F32), pltpu.VMEM((hh, 1), F32)]),
        out_shape=jax.ShapeDtypeStruct((n, hh, hd), F32),
        compiler_params=_cp("parallel", "arbitrary"),
        name="fox_decode",
    )(page_table, q.reshape(n, hh, hd).astype(BF16), k_new.reshape(n, kvh, hd), v_new.reshape(n, kvh, hd),
      logf_new.reshape(n, hh, 1), *([ck] * pp), *([cv] * pp), *([cf] * pp))
    return out.reshape(n, hh * hd).astype(BF16)


def _mla_decode_kernel(pt_ref, *refs, pp, scale):
    del pt_ref
    ql_ref, qr_ref, cnew_ref, rnew_ref = refs[:4]
    c_refs = refs[4:4 + pp]
    r_refs = refs[4 + pp:4 + 2 * pp]
    o_ref = refs[4 + 2 * pp]
    m_ref, l_ref, acc_ref = refs[5 + 2 * pp:]
    step = pl.program_id(1)
    ps = c_refs[0].shape[1]

    @pl.when(step == 0)
    def _():
        m_ref[...] = jnp.full(m_ref.shape, -jnp.inf, F32)
        l_ref[...] = jnp.zeros(l_ref.shape, F32)
        acc_ref[...] = jnp.zeros(acc_ref.shape, F32)

    ql = ql_ref[0]
    qr = qr_ref[0]
    cbs, scores = [], []
    for i in range(pp):
        cb = c_refs[i][0].astype(BF16)
        rb = r_refs[i][0].astype(BF16)
        s = lax.dot_general(ql, cb, NT, preferred_element_type=F32)
        scores.append((s + lax.dot_general(qr, rb, NT, preferred_element_type=F32)) * scale)
        cbs.append(cb)
    s_all = jnp.concatenate(scores, axis=1)
    m_new, corr, p_all, l_new = _softmax_update(s_all, m_ref[...], l_ref[...])
    m_ref[...] = m_new
    l_ref[...] = l_new
    pb = p_all.astype(BF16)
    acc = acc_ref[...] * corr
    for i in range(pp):
        acc = acc + jnp.dot(pb[:, i * ps:(i + 1) * ps], cbs[i], preferred_element_type=F32)
    acc_ref[...] = acc

    @pl.when(step == pl.num_programs(1) - 1)
    def _():
        cn = cnew_ref[0].astype(BF16).astype(F32)
        rn = rnew_ref[0].astype(BF16).astype(F32)
        s_new = (jnp.sum(ql.astype(F32) * cn, axis=1, keepdims=True)
                 + jnp.sum(qr.astype(F32) * rn, axis=1, keepdims=True)) * scale
        m_fin, corr_fin, p_new, l_fin = _softmax_update(s_new, m_ref[...], l_ref[...])
        acc_fin = acc_ref[...] * corr_fin + p_new.astype(BF16).astype(F32) * cn
        o_ref[0] = (acc_fin / l_fin).astype(o_ref.dtype)


def _mla_sample_attention(q_lat, q_rot, ckv_new, kpe_new, cache_ckv, cache_kpe, layer, page_table, heads, scale):
    n = q_lat.shape[0]
    _, pool, ps, kl = cache_ckv.shape
    rr = cache_kpe.shape[-1]
    npg = page_table.shape[1]
    pp = _pick(npg, (16, 8, 4, 2, 1))
    cc = cache_ckv.reshape(-1, ps, kl)
    cr = cache_kpe.reshape(-1, ps, rr)
    base = layer * pool

    def page_map(i):
        return lambda s, p, pt: (base + pt[s, p * pp + i], 0, 0)

    seq3 = lambda s, p, pt: (s, 0, 0)
    in_specs = [pl.BlockSpec((1, heads, kl), seq3), pl.BlockSpec((1, heads, rr), seq3),
                pl.BlockSpec((1, 1, kl), seq3), pl.BlockSpec((1, 1, rr), seq3)]
    in_specs += [pl.BlockSpec((1, ps, kl), page_map(i)) for i in range(pp)]
    in_specs += [pl.BlockSpec((1, ps, rr), page_map(i)) for i in range(pp)]
    out = pl.pallas_call(
        functools.partial(_mla_decode_kernel, pp=pp, scale=scale),
        grid_spec=pltpu.PrefetchScalarGridSpec(
            num_scalar_prefetch=1,
            grid=(n, npg // pp),
            in_specs=in_specs,
            out_specs=pl.BlockSpec((1, heads, kl), seq3),
            scratch_shapes=[pltpu.VMEM((heads, 1), F32), pltpu.VMEM((heads, 1), F32),
                            pltpu.VMEM((heads, kl), F32)]),
        out_shape=jax.ShapeDtypeStruct((n, heads, kl), BF16),
        compiler_params=_cp("parallel", "arbitrary"),
        name="mla_decode",
    )(page_table, q_lat.reshape(n, heads, kl), q_rot.reshape(n, heads, rr),
      ckv_new.reshape(n, 1, kl), kpe_new.reshape(n, 1, rr), *([cc] * pp), *([cr] * pp))
    return out.reshape(n, heads * kl)


def _mla_norm_kernel(z_ref, gq_ref, gkv_ref, cq_ref, ckv_ref, ckvb_ref, *, ql):
    z = z_ref[...]
    zq, zk = z[:, :ql], z[:, ql:]
    cq = zq * lax.rsqrt(jnp.mean(zq * zq, axis=-1, keepdims=True) + RMS_EPS) * gq_ref[...]
    ck = zk * lax.rsqrt(jnp.mean(zk * zk, axis=-1, keepdims=True) + RMS_EPS) * gkv_ref[...]
    cq_ref[...] = cq.astype(BF16)
    ckv_ref[...] = ck
    ckvb_ref[...] = ck.astype(BF16)


def _mla_norm(z, g_q, g_kv):
    m, w = z.shape
    ql = g_q.shape[0]
    kl = g_kv.shape[0]
    tm = _pick(m, (512, 256, 128))
    return pl.pallas_call(
        functools.partial(_mla_norm_kernel, ql=ql),
        grid=(m // tm,),
        in_specs=[pl.BlockSpec((tm, w), lambda i: (i, 0)),
                  pl.BlockSpec((1, ql), lambda i: (0, 0)),
                  pl.BlockSpec((1, kl), lambda i: (0, 0))],
        out_specs=[pl.BlockSpec((tm, ql), lambda i: (i, 0)),
                   pl.BlockSpec((tm, kl), lambda i: (i, 0)),
                   pl.BlockSpec((tm, kl), lambda i: (i, 0))],
        out_shape=[jax.ShapeDtypeStruct((m, ql), BF16), jax.ShapeDtypeStruct((m, kl), F32),
                   jax.ShapeDtypeStruct((m, kl), BF16)],
        compiler_params=_cp("parallel"),
        name="mla_norm",
    )(z, g_q.astype(F32).reshape(1, ql), g_kv.astype(F32).reshape(1, kl))


def _rope_tables(pos, half, reps):
    inv = ROPE_BASE ** (-jnp.arange(half, dtype=F32) / half)
    ang = pos[:, None] * inv
    cos, sin = jnp.cos(ang), jnp.sin(ang)
    return (jnp.tile(jnp.concatenate([cos, cos], axis=-1), (1, reps)),
            jnp.tile(jnp.concatenate([-sin, sin], axis=-1), (1, reps)))


def _swap_halves(w, width):
    k, n = w.shape
    return w.reshape(k, n // width, 2, width // 2)[:, :, ::-1, :].reshape(k, n)


def _sort_desc(v):
    v = list(v)
    n = len(v)
    k = 2
    while k <= n:
        j = k // 2
        while j >= 1:
            for i in range(n):
                l = i ^ j
                if l > i:
                    hi, lo = jnp.maximum(v[i], v[l]), jnp.minimum(v[i], v[l])
                    v[i], v[l] = (hi, lo) if (i & k) == 0 else (lo, hi)
            j //= 2
        k *= 2
    return v


def _merge_desc(v):
    v = list(v)
    j = len(v) // 2
    while j >= 1:
        for i in range(len(v)):
            l = i ^ j
            if l > i:
                v[i], v[l] = jnp.maximum(v[i], v[l]), jnp.minimum(v[i], v[l])
        j //= 2
    return v


def _top_of_two(a, b):
    n = len(a)
    return _merge_desc([jnp.maximum(a[i], b[n - 1 - i]) for i in range(n)])


def _fold_sublanes(v):
    shift = v[0].shape[0] // 2
    while shift >= 1:
        v = _top_of_two(v, [pltpu.roll(x, shift, 0) for x in v])
        shift //= 2
    return v


def _peer_route_kernel(qp_ref, sk_ref, s1_ref, c1_ref, s2_ref, e2_ref, tau_ref, *, heads, nkeys, topk):
    tt = qp_ref.shape[0]
    sub = lax.broadcasted_iota(jnp.int32, (SUBLANE, tt), 0)
    for h in range(heads):
        sts, vals = [], []
        for c in range(2):
            qh = qp_ref[:, (2 * h + c) * nkeys:(2 * h + c + 1) * nkeys]
            st = lax.dot_general(sk_ref[c, h], qh, NT, preferred_element_type=F32, precision=HI)
            sts.append(st)
            slabs = [st[SUBLANE * i:SUBLANE * (i + 1), :] for i in range(nkeys // SUBLANE)]
            vals.append(_fold_sublanes(_sort_desc(slabs)))
        v1, v2 = vals
        cands = []
        for half in range(topk // SUBLANE):
            b = v2[half * SUBLANE + SUBLANE - 1]
            for s in range(SUBLANE - 2, -1, -1):
                b = jnp.where(sub == s, v2[half * SUBLANE + s], b)
            cands.append([a + b for a in v1])
        top = cands[0]
        for other in cands[1:]:
            top = _top_of_two(top, other)
        top = _fold_sublanes(top)
        mx = top[0][0:1]
        z = jnp.zeros_like(mx)
        for r in range(topk):
            z = z + jnp.exp(top[r][0:1] - mx)
        s1_ref[h] = sts[0]
        s2_ref[h] = sts[1]
        c1_ref[h] = jnp.exp(sts[0] - v1[0][0:1]) / z
        e2_ref[h] = jnp.exp(sts[1] - v2[0][0:1])
        tau_ref[h:h + 1, :] = top[topk - 1][0:1]


def _peer_dense_kernel(h_ref, u_ref, v_ref, s1_ref, c1_ref, s2_ref, e2_ref, tau_ref, o_ref, ht_ref,
                       *, heads, nkeys):
    e = pl.program_id(1)
    eb = u_ref.shape[0]
    tt = h_ref.shape[0]
    per = eb // nkeys

    @pl.when(e == 0)
    def _():
        ht_ref[...] = h_ref[...].astype(F32).T.astype(BF16)
        o_ref[...] = jnp.zeros(o_ref.shape, F32)

    st = jnp.dot(u_ref[...], ht_ref[...], preferred_element_type=F32)
    act = jax.nn.gelu(st)
    parts = []
    for al in range(per):
        a = e * per + al
        w = jnp.zeros((nkeys, tt), F32)
        for h in range(heads):
            sc = s2_ref[h] + s1_ref[h, pl.ds(a, 1), :]
            w = w + jnp.where(sc >= tau_ref[h:h + 1, :], e2_ref[h] * c1_ref[h, pl.ds(a, 1), :], 0.0)
        parts.append(w * act[al * nkeys:(al + 1) * nkeys])
    pt = parts[0] if len(parts) == 1 else jnp.concatenate(parts, axis=0)
    o_ref[...] += jnp.dot(pt.T.astype(BF16), v_ref[...], preferred_element_type=F32)


def _peer_ffn(h, w_q, subkeys, u_tab, v_tab, layer):
    m, d = h.shape
    _, heads, nkeys, kd2 = subkeys.shape
    assert nkeys == LANE and kd2 == LANE, "product-key halves are laid out on one 128-lane tile"
    assert nkeys // SUBLANE == PEER_TOPK, "the top-k network keeps one value per slab of keys"
    ne = u_tab.shape[1]
    qp = _mm(h, w_q, layer=layer)
    tt = _pick(m, (256, 128))
    tab = jax.ShapeDtypeStruct((heads, nkeys, m), F32)
    tspec = pl.BlockSpec((heads, nkeys, tt), lambda i: (0, 0, i))
    s1, c1, s2, e2, tau = pl.pallas_call(
        functools.partial(_peer_route_kernel, heads=heads, nkeys=nkeys, topk=PEER_TOPK),
        grid=(m // tt,),
        in_specs=[pl.BlockSpec((tt, qp.shape[1]), lambda i: (i, 0)),
                  pl.BlockSpec((2, heads, nkeys, kd2), lambda i: (0, 0, 0, 0))],
        out_specs=[tspec, tspec, tspec, tspec, pl.BlockSpec((heads, tt), lambda i: (0, i))],
        out_shape=[tab, tab, tab, tab, jax.ShapeDtypeStruct((heads, m), F32)],
        compiler_params=_cp("parallel"),
        name="peer_route",
    )(qp, subkeys.astype(F32))
    td = _pick(m, (512, 256, 128))
    eb = _pick(ne, (512, 256, 128))
    once = pl.Buffered(1)
    dspec = pl.BlockSpec((heads, nkeys, td), lambda i, e: (0, 0, i), pipeline_mode=once)
    return pl.pallas_call(
        functools.partial(_peer_dense_kernel, heads=heads, nkeys=nkeys),
        grid=(m // td, ne // eb),
        in_specs=[pl.BlockSpec((td, d), lambda i, e: (i, 0), pipeline_mode=once),
                  pl.BlockSpec((None, eb, d), lambda i, e: (layer, e, 0)),
                  pl.BlockSpec((None, eb, d), lambda i, e: (layer, e, 0)),
                  dspec, dspec, dspec, dspec,
                  pl.BlockSpec((heads, td), lambda i, e: (0, i), pipeline_mode=once)],
        out_specs=pl.BlockSpec((td, d), lambda i, e: (i, 0)),
        out_shape=jax.ShapeDtypeStruct((m, d), F32),
        scratch_shapes=[pltpu.VMEM((d, td), BF16)],
        compiler_params=_cp("parallel", "arbitrary"),
        name="peer_dense",
    )(h, u_tab, v_tab, s1, c1, s2, e2, tau)


def kernel(x_prompt, x_sample, state_s5_re, state_s5_im, cache_fox_k, cache_fox_v, cache_fox_logf, cache_mla_ckv, cache_mla_kpe, page_table, c_prompt, c_sample, ada_w, ada_b, ln_g, ln_b, s5_w_in, s5_lam_re, s5_lam_im, s5_log_step, s5_b_re, s5_b_im, s5_c_re, s5_c_im, s5_d, s5_w_glu, fox_w_in, fox_b_f, fox_w_o, mla_w_in, mla_q_norm, mla_kv_norm, mla_w_uq, mla_w_ukv, mla_w_o, peer_w_q, peer_subkeys, peer_u, peer_v):
    n_p, t_p, dm = x_prompt.shape
    n_s, t_s, _ = x_sample.shape
    assert t_s == 1, "the sample group decodes one token per sequence"
    depth = ada_w.shape[0]
    alpha = (2 * depth) ** 0.25
    past_len = page_table.shape[1] * cache_fox_k.shape[2]
    fox_heads = fox_b_f.shape[-1]
    fox_kvh, fox_hd = cache_fox_k.shape[3], cache_fox_k.shape[4]
    mla_ql, mla_kl, mla_r = mla_q_norm.shape[-1], mla_kv_norm.shape[-1], cache_mla_kpe.shape[-1]
    mla_heads = (mla_w_uq.shape[-1] - mla_w_ukv.shape[-1] + mla_w_o.shape[-2]) // mla_r
    mla_dv = mla_w_o.shape[-2] // mla_heads
    mla_dn = mla_w_ukv.shape[-1] // mla_heads - mla_dv
    assert mla_dn == mla_dv, "k_nope and v column blocks of w_ukv are addressed with one block width"

    s5_w_in_b, s5_w_glu_b = s5_w_in.astype(BF16), s5_w_glu.astype(BF16)
    fox_w_in_b, fox_w_o_b = fox_w_in.astype(BF16), fox_w_o.astype(BF16)
    mla_w_in_b, mla_w_uq_b = mla_w_in.astype(BF16), mla_w_uq.astype(BF16)
    mla_w_ukv_b, mla_w_o_b = mla_w_ukv.astype(BF16), mla_w_o.astype(BF16)
    peer_w_q_b, peer_u_b, peer_v_b = peer_w_q.astype(BF16), peer_u.astype(BF16), peer_v.astype(BF16)

    rows = n_p + n_s
    rpad = -(-rows // SUBLANE) * SUBLANE
    c_all = jnp.pad(jnp.concatenate([c_prompt, c_sample], axis=0).astype(F32), ((0, rpad - rows), (0, 0)))
    mod = _modulation_all(c_all, ada_w.astype(F32), ada_b.astype(F32))

    def mods(i, s):
        m = mod[2 * i + s]
        mp = m[:n_p, None, :]
        ms = m[n_p:rows][None]
        cut = lambda a: (a[..., :dm], a[..., dm:2 * dm], a[..., 2 * dm:])
        return cut(mp), cut(ms)

    pos_p = jnp.arange(t_p, dtype=F32)
    pos_s = past_len + jnp.arange(t_s, dtype=F32)
    cos_p, sin_p = _rope_tables(pos_p, mla_r // 2, LANE // mla_r)
    cos_p, sin_p = jnp.tile(cos_p, (n_p, 1)), jnp.tile(sin_p, (n_p, 1))
    cos_s, sin_s = _rope_tables(jnp.tile(pos_s, (n_s,)), mla_r // 2, LANE // mla_r)

    yp, ys = x_prompt.astype(F32), x_sample.astype(F32).reshape(1, n_s, dm)
    (shp, scp, _), (shs, scs, _) = mods(0, 0)
    hp = _ln_mod(yp, scale=scp, shift=shp, alpha=alpha)
    hs = _ln_mod(ys, scale=scs, shift=shs, alpha=alpha)

    out = {k: [] for k in ("s5r_p", "s5i_p", "s5r_s", "s5i_s", "fk_p", "fv_p", "fl_p", "fk_s", "fv_s", "fl_s",
                           "ckv_p", "kpe_p", "ckv_s", "kpe_s")}
    for i in range(depth):
        kind, j = i % 3, i // 3
        (_, _, gate_p), (_, _, gate_s) = mods(i, 0)
        if kind == 0:
            prm = (s5_lam_re[j], s5_lam_im[j], s5_log_step[j], s5_b_re[j], s5_b_im[j], s5_c_re[j], s5_c_im[j])
            op, sr, si = _s5_prompt(hp, prm, s5_w_in_b, s5_d[j], s5_w_glu_b, j)
            out["s5r_p"].append(sr)
            out["s5i_p"].append(si)
            os_, sr, si = _s5_sample(hs.reshape(n_s, 1, dm), state_s5_re[j], state_s5_im[j], prm, s5_w_in_b,
                                     s5_d[j], s5_w_glu_b, j)
            out["s5r_s"].append(sr)
            out["s5i_s"].append(si)
            os_ = os_.reshape(1, n_s, dm)
        elif kind == 1:
            nq, nkv = fox_heads * fox_hd, fox_kvh * fox_hd
            w_f = fox_w_in_b[j][:, nq + 2 * nkv:]
            h2 = hp.reshape(n_p * t_p, dm)
            q = _mm(h2, fox_w_in_b, (BF16,), layer=j, col0=0, ncols=nq)
            kv, kvb = _mm(h2, fox_w_in_b, (F32, BF16), layer=j, col0=nq, ncols=2 * nkv)
            lf = _mm_logsig(h2, w_f, fox_b_f[j])
            f_cum = _cumsum_time(lf.reshape(n_p, t_p, fox_heads))
            o = _fox_prompt_attention(q.reshape(n_p, t_p, nq), kvb.reshape(n_p, t_p, 2 * nkv), f_cum,
                                      fox_kvh, fox_hd)
            op = _mm(o.reshape(n_p * t_p, nq), fox_w_o_b, layer=j).reshape(n_p, t_p, dm)
            out["fk_p"].append(kv[:, :nkv].reshape(n_p, t_p, fox_kvh, fox_hd))
            out["fv_p"].append(kv[:, nkv:].reshape(n_p, t_p, fox_kvh, fox_hd))
            out["fl_p"].append(lf.reshape(n_p, t_p, fox_heads))
            h2 = hs.reshape(n_s, dm)
            q = _mm(h2, fox_w_in_b, layer=j, col0=0, ncols=nq)
            kv = _mm(h2, fox_w_in_b, layer=j, col0=nq, ncols=2 * nkv)
            lf = _mm_logsig(h2, w_f, fox_b_f[j])
            o = _fox_sample_attention(q, kv[:, :nkv], kv[:, nkv:], lf, cache_fox_k, cache_fox_v,
                                      cache_fox_logf, j, page_table)
            os_ = _mm(o, fox_w_o_b, layer=j).reshape(1, n_s, dm)
            out["fk_s"].append(kv[:, :nkv].reshape(n_s, 1, fox_kvh, fox_hd))
            out["fv_s"].append(kv[:, nkv:].reshape(n_s, 1, fox_kvh, fox_hd))
            out["fl_s"].append(lf.reshape(n_s, 1, fox_heads))
        else:
            nlat = mla_ql + mla_kl
            w_pe = mla_w_in_b[j][:, nlat:]
            w_pe_rot = _swap_halves(w_pe, mla_r)
            wq = mla_w_uq_b[j].reshape(mla_ql, mla_heads, mla_dn + mla_r)
            wq_n = wq[:, :, :mla_dn].reshape(mla_ql, mla_heads * mla_dn)
            wq_r = wq[:, :, mla_dn:].reshape(mla_ql, mla_heads * mla_r)
            wq_r_rot = _swap_halves(wq_r, mla_r)

            def project(h2, cos, sin):
                z = _mm(h2, mla_w_in_b, layer=j, col0=0, ncols=nlat)
                cq, ckv, ckvb = _mla_norm(z, mla_q_norm[j], mla_kv_norm[j])
                kpe = _mm_rope(h2, w_pe, w_pe_rot, cos[:, :mla_r], sin[:, :mla_r], (F32,))
                qn = _mm(cq, wq_n, (BF16,))
                qr = _mm_rope(cq, wq_r, wq_r_rot, cos, sin, (BF16,))
                return qn, qr, ckv, ckvb, kpe

            qn, qr, ckv, ckvb, kpe = project(hp.reshape(n_p * t_p, dm), cos_p, sin_p)
            kvb = _mm(ckvb, mla_w_ukv_b, (BF16,), layer=j)
            qr4 = qr.reshape(n_p, t_p, mla_heads, mla_r).transpose(0, 2, 1, 3)
            o = _mla_prompt_attention(qn.reshape(n_p, t_p, -1), qr4, kvb.reshape(n_p, t_p, -1),
                                      kpe.astype(BF16).reshape(n_p, t_p, mla_r), mla_heads, mla_dn, mla_r, mla_dv)
            op = _mm(o.reshape(n_p * t_p, -1), mla_w_o_b, layer=j).reshape(n_p, t_p, dm)
            out["ckv_p"].append(ckv.reshape(n_p, t_p, mla_kl))
            out["kpe_p"].append(kpe.reshape(n_p, t_p, mla_r))
            qn, qr, ckv, ckvb, kpe = project(hs.reshape(n_s, dm), cos_s, sin_s)
            q_lat = _head_mm(qn, mla_w_ukv_b, j, mla_heads, 0, True, BF16)
            a = _mla_sample_attention(q_lat, qr, ckv, kpe, cache_mla_ckv, cache_mla_kpe, j, page_table, mla_heads,
                                      (mla_dn + mla_r) ** -0.5)
            o = _head_mm(a, mla_w_ukv_b, j, mla_heads, 1, False, BF16)
            os_ = _mm(o, mla_w_o_b, layer=j).reshape(1, n_s, dm)
            out["ckv_s"].append(ckv.reshape(n_s, 1, mla_kl))
            out["kpe_s"].append(kpe.reshape(n_s, 1, mla_r))
        (shp, scp, gate2_p), (shs, scs, gate2_s) = mods(i, 1)
        yp, hp = _ln_mod(yp, op, gate_p, ln_g[i, 0], ln_b[i, 0], scp, shp, alpha=alpha)
        ys, hs = _ln_mod(ys, os_, gate_s, ln_g[i, 0], ln_b[i, 0], scs, shs, alpha=alpha)
        fp = _peer_ffn(hp.reshape(n_p * t_p, dm), peer_w_q_b, peer_subkeys[i], peer_u_b, peer_v_b, i)
        fs = _peer_ffn(hs.reshape(n_s, dm), peer_w_q_b, peer_subkeys[i], peer_u_b, peer_v_b, i)
        fp, fs = fp.reshape(n_p, t_p, dm), fs.reshape(1, n_s, dm)
        if i + 1 < depth:
            (shp, scp, _), (shs, scs, _) = mods(i + 1, 0)
            yp, hp = _ln_mod(yp, fp, gate2_p, ln_g[i, 1], ln_b[i, 1], scp, shp, alpha=alpha)
            ys, hs = _ln_mod(ys, fs, gate2_s, ln_g[i, 1], ln_b[i, 1], scs, shs, alpha=alpha)
        else:
            yp = _ln_mod(yp, fp, gate2_p, ln_g[i, 1], ln_b[i, 1], alpha=alpha)
            ys = _ln_mod(ys, fs, gate2_s, ln_g[i, 1], ln_b[i, 1], alpha=alpha)

    st = lambda k: jnp.stack(out[k])
    return (yp, ys.reshape(n_s, 1, dm),
            st("s5r_p"), st("s5i_p"), st("fk_p"), st("fv_p"), st("fl_p"), st("ckv_p"), st("kpe_p"),
            st("s5r_s"), st("s5i_s"), st("fk_s"), st("fv_s"), st("fl_s"), st("ckv_s"), st("kpe_s"))
```

```python
import functools
import math

import jax
import jax.numpy as jnp
from jax import lax
from jax.experimental import pallas as pl
from jax.experimental.pallas import tpu as pltpu

F32 = jnp.float32
BF16 = jnp.bfloat16
HI = lax.Precision.HIGHEST

LANE = 128
SUBLANE = 8
VMEM_LIMIT = 56 * 1024 * 1024
LN_EPS = 1e-5
RMS_EPS = 1e-6
ROPE_BASE = 10000.0
LOG2E = math.log2(math.e)
S5_CHUNK = 8
PEER_TOPK = 16
NT = (((1,), (1,)), ((), ()))


def _cp(*sem):
    return pltpu.CompilerParams(dimension_semantics=sem, vmem_limit_bytes=VMEM_LIMIT)


def _pick(n, cands):
    for c in cands:
        if c <= n and n % c == 0:
            return c
    return n


def _wspec(w, k, tn, layer, col0, ngrid):
    off = col0 // tn
    assert col0 % tn == 0
    if w.ndim == 3:
        if ngrid == 1:
            return pl.BlockSpec((None, k, tn), lambda j: (layer, 0, off + j))
        return pl.BlockSpec((None, k, tn), lambda i, j: (layer, 0, off + j))
    if ngrid == 1:
        return pl.BlockSpec((k, tn), lambda j: (0, off + j))
    return pl.BlockSpec((k, tn), lambda i, j: (0, off + j))


def _mm_kernel(x_ref, w_ref, *o_refs):
    acc = jnp.dot(x_ref[...], w_ref[...], preferred_element_type=F32)
    for o_ref in o_refs:
        o_ref[...] = acc.astype(o_ref.dtype)


def _mm(x, w, out_dtypes=(F32,), layer=0, col0=0, ncols=None):
    m, k = x.shape
    n = w.shape[-1] - col0 if ncols is None else ncols
    tm = _pick(m, (1024, 512, 256, 128))
    tn = _pick(math.gcd(n, col0) if col0 else n, (1024, 512, 256, 128))
    outs = pl.pallas_call(
        _mm_kernel,
        grid=(m // tm, n // tn),
        in_specs=[pl.BlockSpec((tm, k), lambda i, j: (i, 0)), _wspec(w, k, tn, layer, col0, 2)],
        out_specs=[pl.BlockSpec((tm, tn), lambda i, j: (i, j)) for _ in out_dtypes],
        out_shape=[jax.ShapeDtypeStruct((m, n), d) for d in out_dtypes],
        compiler_params=_cp("parallel", "arbitrary"),
        name="mm",
    )(x, w)
    return outs[0] if len(out_dtypes) == 1 else outs


def _glu_kernel(x_ref, wv_ref, wg_ref, o_ref):
    x = x_ref[...]
    val = jnp.dot(x, wv_ref[...], preferred_element_type=F32)
    gate = jnp.dot(x, wg_ref[...], preferred_element_type=F32)
    o_ref[...] = val * jax.nn.sigmoid(gate)


def _mm_glu(x, w, layer):
    m, k = x.shape
    n = w.shape[-1] // 2
    tm = _pick(m, (1024, 512, 256, 128))
    tn = _pick(n, (512, 256, 128))
    return pl.pallas_call(
        _glu_kernel,
        grid=(m // tm, n // tn),
        in_specs=[pl.BlockSpec((tm, k), lambda i, j: (i, 0)),
                  _wspec(w, k, tn, layer, 0, 2), _wspec(w, k, tn, layer, n, 2)],
        out_specs=pl.BlockSpec((tm, tn), lambda i, j: (i, j)),
        out_shape=jax.ShapeDtypeStruct((m, n), F32),
        compiler_params=_cp("parallel", "arbitrary"),
        name="mm_glu",
    )(x, w, w)


def _logsig_kernel(x_ref, w_ref, b_ref, o_ref):
    z = jnp.dot(x_ref[...], w_ref[...], preferred_element_type=F32) + b_ref[...]
    o_ref[...] = jnp.minimum(z, 0.0) - jnp.log1p(jnp.exp(-jnp.abs(z)))


def _mm_logsig(x, w, b):
    m, k = x.shape
    n = w.shape[1]
    tm = _pick(m, (1024, 512, 256, 128))
    return pl.pallas_call(
        _logsig_kernel,
        grid=(m // tm,),
        in_specs=[pl.BlockSpec((tm, k), lambda i: (i, 0)),
                  pl.BlockSpec((k, n), lambda i: (0, 0)),
                  pl.BlockSpec((1, n), lambda i: (0, 0))],
        out_specs=pl.BlockSpec((tm, n), lambda i: (i, 0)),
        out_shape=jax.ShapeDtypeStruct((m, n), F32),
        compiler_params=_cp("parallel"),
        name="mm_logsig",
    )(x, w, b.reshape(1, n).astype(F32))


def _rope_kernel(x_ref, w_ref, wr_ref, cos_ref, sin_ref, *o_refs):
    x = x_ref[...]
    a = jnp.dot(x, w_ref[...], preferred_element_type=F32)
    b = jnp.dot(x, wr_ref[...], preferred_element_type=F32)
    reps = a.shape[1] // cos_ref.shape[1]
    cos, sin = cos_ref[...], sin_ref[...]
    if reps > 1:
        cos = jnp.concatenate([cos] * reps, axis=1)
        sin = jnp.concatenate([sin] * reps, axis=1)
    out = a * cos + b * sin
    for o_ref in o_refs:
        o_ref[...] = out.astype(o_ref.dtype)


def _mm_rope(x, w, w_rot, cos, sin, out_dtypes):
    m, k = x.shape
    n = w.shape[1]
    period = cos.shape[1]
    tm = _pick(m, (1024, 512, 256, 128))
    tn = _pick(n, (1024, 512, 256, 128)) if n % period == 0 and period % LANE == 0 else n
    outs = pl.pallas_call(
        _rope_kernel,
        grid=(m // tm, n // tn),
        in_specs=[pl.BlockSpec((tm, k), lambda i, j: (i, 0)),
                  pl.BlockSpec((k, tn), lambda i, j: (0, j)),
                  pl.BlockSpec((k, tn), lambda i, j: (0, j)),
                  pl.BlockSpec((tm, period), lambda i, j: (i, 0)),
                  pl.BlockSpec((tm, period), lambda i, j: (i, 0))],
        out_specs=[pl.BlockSpec((tm, tn), lambda i, j: (i, j)) for _ in out_dtypes],
        out_shape=[jax.ShapeDtypeStruct((m, n), d) for d in out_dtypes],
        compiler_params=_cp("parallel", "arbitrary"),
        name="mm_rope",
    )(x, w, w_rot, cos, sin)
    return outs[0] if len(out_dtypes) == 1 else outs


def _headmm_kernel(x_ref, w_ref, o_ref, *, trans_w):
    if trans_w:
        acc = lax.dot_general(x_ref[...], w_ref[...], NT, preferred_element_type=F32)
    else:
        acc = jnp.dot(x_ref[...], w_ref[...], preferred_element_type=F32)
    o_ref[...] = acc.astype(o_ref.dtype)


def _head_mm(x, w, layer, heads, w_off, trans_w, out_dtype):
    m = x.shape[0]
    dx = x.shape[1] // heads
    r = w.shape[1]
    dw = w.shape[2] // (2 * heads)
    do = r if trans_w else dw
    return pl.pallas_call(
        functools.partial(_headmm_kernel, trans_w=trans_w),
        grid=(heads,),
        in_specs=[pl.BlockSpec((m, dx), lambda h: (0, h)),
                  pl.BlockSpec((None, r, dw), lambda h: (layer, 0, 2 * h + w_off))],
        out_specs=pl.BlockSpec((m, do), lambda h: (0, h)),
        out_shape=jax.ShapeDtypeStruct((m, heads * do), out_dtype),
        compiler_params=_cp("parallel"),
        name="head_mm",
    )(x, w)


def _mod_kernel(c_ref, w_ref, b_ref, o_ref):
    c = c_ref[...]
    a = (c * jax.nn.sigmoid(c)).astype(BF16)
    o_ref[0] = jnp.dot(a, w_ref[0].astype(BF16), preferred_element_type=F32) + b_ref[0]


def _modulation_all(c_all, ada_w, ada_b):
    r, d = c_all.shape
    depth = ada_w.shape[0]
    s = depth * 2
    n3 = ada_w.shape[-1]
    w = ada_w.reshape(s, d, n3)
    b = ada_b.reshape(s, 1, n3)
    tn = _pick(n3, (512, 256, 128))
    return pl.pallas_call(
        _mod_kernel,
        grid=(s, n3 // tn),
        in_specs=[pl.BlockSpec((r, d), lambda i, j: (0, 0)),
                  pl.BlockSpec((1, d, tn), lambda i, j: (i, 0, j)),
                  pl.BlockSpec((1, 1, tn), lambda i, j: (i, 0, j))],
        out_specs=pl.BlockSpec((1, r, tn), lambda i, j: (i, 0, j)),
        out_shape=jax.ShapeDtypeStruct((s, r, n3), F32),
        compiler_params=_cp("parallel", "arbitrary"),
        name="modulation",
    )(c_all, w, b)


def _ln_mod_kernel(*refs, alpha, do_ln, do_mod):
    it = iter(refs)
    x_ref = next(it)
    if do_ln:
        f_ref, gate_ref, g_ref, b_ref = next(it), next(it), next(it), next(it)
    if do_mod:
        scale_ref, shift_ref = next(it), next(it)
    if do_ln:
        y_ref = next(it)
    if do_mod:
        h_ref = next(it)
    y = x_ref[0]
    if do_ln:
        xf = alpha * y + (1.0 + gate_ref[0]) * f_ref[0]
        mu = jnp.mean(xf, axis=-1, keepdims=True)
        xc = xf - mu
        var = jnp.mean(xc * xc, axis=-1, keepdims=True)
        y = xc * lax.rsqrt(var + LN_EPS) * g_ref[...] + b_ref[...]
        y_ref[0] = y
    if do_mod:
        h_ref[0] = (y * (1.0 + scale_ref[0]) + shift_ref[0]).astype(BF16)


def _ln_mod(x, f=None, gate=None, g=None, b=None, scale=None, shift=None, *, alpha):
    n, t, d = x.shape
    do_ln, do_mod = f is not None, scale is not None
    tt = _pick(t, (256, 128))
    xspec = pl.BlockSpec((1, tt, d), lambda i, j: (i, j, 0))

    def mspec(a):
        if a.shape[1] == 1:
            return pl.BlockSpec((1, 1, d), lambda i, j: (i, 0, 0))
        return xspec

    vspec = pl.BlockSpec((1, d), lambda i, j: (0, 0))
    args, specs, oshapes, ospecs = [x], [xspec], [], []
    if do_ln:
        args += [f, gate, g.reshape(1, d), b.reshape(1, d)]
        specs += [xspec, mspec(gate), vspec, vspec]
        oshapes.append(jax.ShapeDtypeStruct((n, t, d), F32))
        ospecs.append(xspec)
    if do_mod:
        args += [scale, shift]
        specs += [mspec(scale), mspec(shift)]
        oshapes.append(jax.ShapeDtypeStruct((n, t, d), BF16))
        ospecs.append(xspec)
    outs = pl.pallas_call(
        functools.partial(_ln_mod_kernel, alpha=alpha, do_ln=do_ln, do_mod=do_mod),
        grid=(n, t // tt),
        in_specs=specs, out_specs=ospecs, out_shape=oshapes,
        compiler_params=_cp("parallel", "parallel"),
        name="ln_mod",
    )(*args)
    return outs if len(outs) > 1 else outs[0]


def _s5_discretize(lam_re, lam_im, log_step, b_re, b_im):
    lr, li = lam_re.astype(F32), lam_im.astype(F32)
    dt = jnp.exp(log_step.astype(F32))[:, None]
    mag, ang = jnp.exp(lr * dt), li * dt
    a_re, a_im = mag * jnp.cos(ang), mag * jnp.sin(ang)
    den = lr * lr + li * li
    f_re = ((a_re - 1.0) * lr + a_im * li) / den
    f_im = (a_im * lr - (a_re - 1.0) * li) / den
    br, bi = b_re.astype(F32), b_im.astype(F32)
    bb_re = f_re[..., None] * br - f_im[..., None] * bi
    bb_im = f_re[..., None] * bi + f_im[..., None] * br
    return lr * dt, li * dt, a_re, a_im, bb_re, bb_im


def _s5_powers(ldt_re, ldt_im, ks):
    k = jnp.asarray(ks, F32)[:, None, None]
    mag, ang = jnp.exp(ldt_re * k), ldt_im * k
    return mag * jnp.cos(ang), mag * jnp.sin(ang)


def _block_diag(x, gl):
    *lead, g, a, b = x.shape
    q = g // gl
    xq = x.reshape(*lead, q, gl, a, b)
    eye = jnp.eye(gl, dtype=x.dtype)
    out = xq[..., :, :, :, None, :] * eye[:, None, :, None]
    return out.reshape(*lead, q, gl * a, gl * b)


def _s5_operators(prm, chunk, n_chunks):
    lam_re, lam_im, log_step, b_re, b_im, c_re, c_im = prm
    g, p = lam_re.shape
    c = b_re.shape[-1]
    gl = LANE // c
    q = g // gl
    ldr, ldi, _, _, bb_re, bb_im = _s5_discretize(lam_re, lam_im, log_step, b_re, b_im)
    cr, ci = c_re.astype(F32), c_im.astype(F32)
    ak_re, ak_im = _s5_powers(ldr, ldi, list(range(chunk + 1)))
    m_re = ak_re[..., None] * bb_re - ak_im[..., None] * bb_im
    m_im = ak_re[..., None] * bb_im + ak_im[..., None] * bb_re
    kk = (jnp.einsum("gop,kgpi->kgio", cr, m_re[:chunk], precision=HI)
          - jnp.einsum("gop,kgpi->kgio", ci, m_im[:chunk], precision=HI))
    kc = kk.reshape(chunk, q, gl * c, c).transpose(1, 0, 2, 3)
    rev = jnp.arange(chunk - 1, -1, -1)
    mc = jnp.stack([m_re[rev], m_im[rev]]).transpose(0, 1, 2, 4, 3)
    mc = mc.reshape(2, chunk, q, gl * c, p).transpose(2, 0, 1, 3, 4)
    pr, pi = ak_re[1:], ak_im[1:]
    v_re = cr[None] * pr[:, :, None, :] - ci[None] * pi[:, :, None, :]
    v_im = -cr[None] * pi[:, :, None, :] - ci[None] * pr[:, :, None, :]
    vc = jnp.stack([v_re, v_im]).transpose(0, 1, 2, 4, 3)
    vc = vc.reshape(2, chunk, q, gl * p, c).transpose(2, 0, 1, 3, 4)
    del n_chunks
    exps = [chunk * (i + 1) for i in range(SUBLANE)] + [chunk * (1 << j) for j in range(_log2(SUBLANE))]
    ad_re, ad_im = _s5_powers(ldr, ldi, exps)
    ad = jnp.concatenate([ad_re.reshape(len(exps), q, gl * p), ad_im.reshape(len(exps), q, gl * p)], axis=-1)
    rows = -(-len(exps) // SUBLANE) * SUBLANE
    ad = jnp.pad(ad.transpose(1, 0, 2), ((0, 0), (0, rows - len(exps)), (0, 0)))
    return kc, mc, vc, ad


def _s5_step_operators(prm):
    lam_re, lam_im, log_step, b_re, b_im, c_re, c_im = prm
    g, p = lam_re.shape
    gl = LANE // b_re.shape[-1]
    q = g // gl
    _, _, a_re, a_im, bb_re, bb_im = _s5_discretize(lam_re, lam_im, log_step, b_re, b_im)
    bb = jnp.concatenate([_block_diag(bb_re.transpose(0, 2, 1), gl),
                          _block_diag(bb_im.transpose(0, 2, 1), gl)], axis=-1).astype(BF16)
    cr, ci = c_re.astype(F32), c_im.astype(F32)
    cc = jnp.concatenate([_block_diag(cr.transpose(0, 2, 1), gl),
                          _block_diag(-ci.transpose(0, 2, 1), gl)], axis=1).astype(BF16)
    a = jnp.concatenate([a_re.reshape(q, 1, gl * p), a_im.reshape(q, 1, gl * p)], axis=-1)
    return bb, cc, a


def _log2(x):
    assert x & (x - 1) == 0, "power of two expected"
    return x.bit_length() - 1


def _lane_tiler(src, dst):
    r = lax.broadcasted_iota(jnp.int32, (src, dst), 0)
    c = lax.broadcasted_iota(jnp.int32, (src, dst), 1)
    return ((c & (src - 1)) == r).astype(BF16)


def _same_block(rows, rblk, cols, cblk):
    r = lax.shift_right_logical(lax.broadcasted_iota(jnp.int32, (rows, cols), 0), _log2(rblk))
    c = lax.shift_right_logical(lax.broadcasted_iota(jnp.int32, (rows, cols), 1), _log2(cblk))
    return r == c


def _s5_scan_kernel(u_ref, kc_ref, mc_ref, vc_ref, ad_ref, d_ref, y_ref, sfin_ref, t_scr, w_scr, v_scr, e_scr,
                    *, n, nk, chunk):
    c = kc_ref.shape[-1]
    p = mc_ref.shape[-1]
    sw = (LANE // c) * p
    tile_c, tile_p = _lane_tiler(c, LANE), _lane_tiler(p, sw)
    mask_t = _same_block(LANE, c, LANE, c)
    mask_w = _same_block(LANE, c, sw, p)
    mask_v = _same_block(sw, p, LANE, c)

    def expand(x, tiler, mask):
        return jnp.where(mask, jnp.dot(x.astype(BF16), tiler, preferred_element_type=F32), 0.0).astype(BF16)

    bd = [expand(kc_ref[0, k], tile_c, mask_t) for k in range(chunk)]
    zero_blk = jnp.zeros((LANE, LANE), BF16)
    for ti in range(chunk):
        rows = slice(ti * LANE, (ti + 1) * LANE)
        for to in range(chunk):
            t_scr[rows, to * LANE:(to + 1) * LANE] = bd[to - ti] if to >= ti else zero_blk
        for ri in range(2):
            w_scr[rows, ri * sw:(ri + 1) * sw] = expand(mc_ref[0, ri, ti], tile_p, mask_w)
    for ri in range(2):
        for to in range(chunk):
            v_scr[ri * sw:(ri + 1) * sw, to * LANE:(to + 1) * LANE] = expand(vc_ref[0, ri, to], tile_c, mask_v)

    u = u_ref[0]
    ub = u.astype(BF16)
    x = jnp.dot(ub, w_scr[...], preferred_element_type=F32)
    half = x.shape[1] // 2
    xr, xi = x[:, :half], x[:, half:]
    nrows = x.shape[0]
    k = lax.rem(lax.broadcasted_iota(jnp.int32, (nrows, 1), 0), nk)
    kin = k & (SUBLANE - 1)
    for j in range(_log2(SUBLANE)):
        d = 1 << j
        ar, ai = ad_ref[0, SUBLANE + j:SUBLANE + j + 1, :half], ad_ref[0, SUBLANE + j:SUBLANE + j + 1, half:]
        keep = kin >= d
        sr = jnp.where(keep, pltpu.roll(xr, d, 0), 0.0)
        si = jnp.where(keep, pltpu.roll(xi, d, 0), 0.0)
        xr, xi = xr + (ar * sr - ai * si), xi + (ar * si + ai * sr)
    e_scr[:, :half] = xr
    e_scr[:, half:] = xi
    pr, pi = ad_ref[0, 0:SUBLANE, :half], ad_ref[0, 0:SUBLANE, half:]
    tiles = nk // SUBLANE
    carries = [None] * n
    for j in range(tiles):
        for s in range(n):
            rows = slice((s * tiles + j) * SUBLANE, (s * tiles + j + 1) * SUBLANE)
            tr, ti = e_scr[rows, :half], e_scr[rows, half:]
            if j > 0:
                cr, ci = carries[s]
                tr, ti = tr + (pr * cr - pi * ci), ti + (pr * ci + pi * cr)
                e_scr[rows, :half] = tr
                e_scr[rows, half:] = ti
            carries[s] = (tr[SUBLANE - 1:SUBLANE], ti[SUBLANE - 1:SUBLANE])
    xr, xi = e_scr[:, :half], e_scr[:, half:]
    keep = k >= 1
    s0 = jnp.concatenate([jnp.where(keep, pltpu.roll(xr, 1, 0), 0.0),
                          jnp.where(keep, pltpu.roll(xi, 1, 0), 0.0)], axis=1).astype(BF16)
    y = jnp.dot(ub, t_scr[...], preferred_element_type=F32)
    y = y + jnp.dot(s0, v_scr[...], preferred_element_type=F32)
    y = y + d_ref[0] * u
    y_ref[0] = jax.nn.gelu(y).astype(BF16)
    for s in range(n):
        last = (s + 1) * nk - 1
        sfin_ref[0, s:s + 1, :] = jnp.concatenate([xr[last:last + 1], xi[last:last + 1]], axis=1)


def _s5_prompt(h, prm, w_in, d_skip, w_glu, layer):
    n, t, dm = h.shape
    g, p = prm[0].shape
    c = prm[3].shape[-1]
    gl = LANE // c
    q = g // gl
    chunk = S5_CHUNK
    nk = t // chunk
    u = _mm(h.reshape(n * t, dm), w_in, layer=layer)
    uc = u.reshape(n, nk, chunk, q, LANE).transpose(3, 0, 1, 2, 4).reshape(q, n * nk, chunk * LANE)
    assert nk % SUBLANE == 0, "the chunk scan works on whole 8-chunk tiles"
    kc, mc, vc, ad = _s5_operators(prm, chunk, nk)
    d_t = jnp.tile(d_skip.astype(F32).reshape(q, 1, LANE), (1, 1, chunk))
    width = chunk * LANE
    sw = 2 * gl * p
    yc, sfin = pl.pallas_call(
        functools.partial(_s5_scan_kernel, n=n, nk=nk, chunk=chunk),
        grid=(q,),
        in_specs=[pl.BlockSpec((1, n * nk, width), lambda i: (i, 0, 0)),
                  pl.BlockSpec((1, chunk, LANE, c), lambda i: (i, 0, 0, 0)),
                  pl.BlockSpec((1, 2, chunk, LANE, p), lambda i: (i, 0, 0, 0, 0)),
                  pl.BlockSpec((1, 2, chunk, gl * p, c), lambda i: (i, 0, 0, 0, 0)),
                  pl.BlockSpec((1, ad.shape[1], sw), lambda i: (i, 0, 0)),
                  pl.BlockSpec((1, 1, width), lambda i: (i, 0, 0))],
        out_specs=[pl.BlockSpec((1, n * nk, width), lambda i: (i, 0, 0)),
                   pl.BlockSpec((1, n, sw), lambda i: (i, 0, 0))],
        out_shape=[jax.ShapeDtypeStruct((q, n * nk, width), BF16),
                   jax.ShapeDtypeStruct((q, n, sw), F32)],
        scratch_shapes=[pltpu.VMEM((width, width), BF16), pltpu.VMEM((width, sw), BF16),
                        pltpu.VMEM((sw, width), BF16), pltpu.VMEM((n * nk, sw), F32)],
        compiler_params=_cp("parallel"),
        name="s5_scan",
    )(uc, kc, mc, vc, ad, d_t)
    y = yc.reshape(q, n, nk, chunk, LANE).transpose(1, 2, 3, 0, 4).reshape(n * t, dm)
    out = _mm_glu(y, w_glu, layer).reshape(n, t, dm)
    sf = sfin.reshape(q, n, 2, gl, p).transpose(2, 1, 0, 3, 4).reshape(2, n, g, p)
    return out, sf[0], sf[1]


def _s5_step_kernel(u_ref, sr_ref, si_ref, bb_ref, cc_ref, a_ref, d_ref, y_ref, or_ref, oi_ref):
    u = u_ref[...]
    bu = jnp.dot(u.astype(BF16), bb_ref[0], preferred_element_type=F32)
    half = bu.shape[1] // 2
    ar, ai = a_ref[0, :, :half], a_ref[0, :, half:]
    sr, si = sr_ref[...], si_ref[...]
    nr = bu[:, :half] + (ar * sr - ai * si)
    ni = bu[:, half:] + (ar * si + ai * sr)
    or_ref[...] = nr
    oi_ref[...] = ni
    s = jnp.concatenate([nr, ni], axis=1).astype(BF16)
    y = jnp.dot(s, cc_ref[0], preferred_element_type=F32) + d_ref[0] * u
    y_ref[...] = jax.nn.gelu(y).astype(BF16)


def _s5_sample(h, s0_re, s0_im, prm, w_in, d_skip, w_glu, layer):
    n, _, dm = h.shape
    g, p = prm[0].shape
    gl = LANE // prm[3].shape[-1]
    q = g // gl
    sw = gl * p
    u = _mm(h.reshape(n, dm), w_in, layer=layer)
    bb, cc, a = _s5_step_operators(prm)
    y, nr, ni = pl.pallas_call(
        _s5_step_kernel,
        grid=(q,),
        in_specs=[pl.BlockSpec((n, LANE), lambda i: (0, i)),
                  pl.BlockSpec((n, sw), lambda i: (0, i)),
                  pl.BlockSpec((n, sw), lambda i: (0, i)),
                  pl.BlockSpec((1, LANE, 2 * sw), lambda i: (i, 0, 0)),
                  pl.BlockSpec((1, 2 * sw, LANE), lambda i: (i, 0, 0)),
                  pl.BlockSpec((1, 1, 2 * sw), lambda i: (i, 0, 0)),
                  pl.BlockSpec((1, 1, LANE), lambda i: (i, 0, 0))],
        out_specs=[pl.BlockSpec((n, LANE), lambda i: (0, i)),
                   pl.BlockSpec((n, sw), lambda i: (0, i)),
                   pl.BlockSpec((n, sw), lambda i: (0, i))],
        out_shape=[jax.ShapeDtypeStruct((n, dm), BF16),
                   jax.ShapeDtypeStruct((n, g * p), F32),
                   jax.ShapeDtypeStruct((n, g * p), F32)],
        compiler_params=_cp("parallel"),
        name="s5_step",
    )(u, s0_re.astype(F32).reshape(n, g * p), s0_im.astype(F32).reshape(n, g * p), bb, cc, a,
      d_skip.astype(F32).reshape(q, 1, LANE))
    out = _mm_glu(y, w_glu, layer).reshape(n, 1, dm)
    return out, nr.reshape(n, g, p), ni.reshape(n, g, p)


def _tri_lower(nrows, ncols, strict):
    r = lax.broadcasted_iota(jnp.int32, (nrows, ncols), 0)
    c = lax.broadcasted_iota(jnp.int32, (nrows, ncols), 1)
    return ((c < r) if strict else (c <= r)).astype(F32)


def _cumsum_kernel(x_ref, o_ref, *, blk):
    t = x_ref.shape[1]
    tri = _tri_lower(blk, blk, strict=False)
    carry = jnp.zeros((1, x_ref.shape[2]), F32)
    for i in range(t // blk):
        xb = x_ref[0, i * blk:(i + 1) * blk, :]
        cb = jnp.dot(tri, xb, preferred_element_type=F32, precision=HI) + carry
        o_ref[0, i * blk:(i + 1) * blk, :] = cb
        carry = cb[blk - 1:blk, :]


def _cumsum_time(x):
    n, t, hh = x.shape
    blk = _pick(t, (128,))
    return pl.pallas_call(
        functools.partial(_cumsum_kernel, blk=blk),
        grid=(n,),
        in_specs=[pl.BlockSpec((1, t, hh), lambda i: (i, 0, 0))],
        out_specs=pl.BlockSpec((1, t, hh), lambda i: (i, 0, 0)),
        out_shape=jax.ShapeDtypeStruct((n, t, hh), F32),
        compiler_params=_cp("parallel"),
        name="cumsum_time",
    )(x)


def _softmax_update(s, m_prev, l_prev):
    m_new = jnp.maximum(m_prev, jnp.max(s, axis=-1, keepdims=True))
    corr = jnp.exp(m_prev - m_new)
    p = jnp.exp(s - m_new)
    return m_new, corr, p, l_prev * corr + jnp.sum(p, axis=-1, keepdims=True)


def _softmax_update2(s, m_prev, l_prev):
    m_new = jnp.maximum(m_prev, jnp.max(s, axis=-1, keepdims=True))
    corr = jnp.exp2(m_prev - m_new)
    p = jnp.exp2(s - m_new)
    return m_new, corr, p, l_prev * corr + jnp.sum(p, axis=-1, keepdims=True)


def _causal_blocks(qi, kv_step, init):
    carry = lax.fori_loop(0, qi, lambda j, c: kv_step(j, c, False), init)
    return kv_step(qi, carry, True)


def _fox_attn_kernel(q_ref, k_ref, v_ref, fq_ref, fk_ref, o_ref, *, gq, tq, scale):
    g = lax.rem(pl.program_id(1), gq)
    qi = pl.program_id(2)
    q = q_ref[0]
    lane = lax.broadcasted_iota(jnp.int32, fq_ref.shape[2:], 1)
    fq = jnp.sum(jnp.where(lane == g, fq_ref[0, 0], 0.0), axis=1, keepdims=True)
    rowpos = qi * tq + lax.broadcasted_iota(jnp.int32, (tq, 1), 0)

    def kv_step(j, carry, diagonal):
        m_prev, l_prev, acc = carry
        start = pl.multiple_of(j * tq, tq)
        s = lax.dot_general(q, k_ref[0, pl.ds(start, tq), :], NT, preferred_element_type=F32) * scale2
        s = s + (fq2 - fk_ref[0, 0, j, pl.ds(g, 1), :] * LOG2E)
        if diagonal:
            s = jnp.where(rowpos >= (j * tq + lax.broadcasted_iota(jnp.int32, (1, tq), 1)), s, -jnp.inf)
        m_new, corr, p, l_new = _softmax_update2(s, m_prev, l_prev)
        acc = acc * corr + jnp.dot(p.astype(BF16), v_ref[0, pl.ds(start, tq), :], preferred_element_type=F32)
        return m_new, l_new, acc

    scale2 = scale * LOG2E
    fq2 = fq * LOG2E
    init = (jnp.full((tq, 1), -jnp.inf, F32), jnp.zeros((tq, 1), F32), jnp.zeros((tq, v_ref.shape[2]), F32))
    _, l, acc = _causal_blocks(qi, kv_step, init)
    o_ref[0] = (acc / l).astype(o_ref.dtype)


def _fox_prompt_attention(q, kvb, f_cum, kvh, hd):
    n, t, qd = q.shape
    heads = qd // hd
    gq = heads // kvh
    tq = _pick(t, (512, 256, 128))
    nq = t // tq
    fq = f_cum.reshape(n, t, kvh, gq).transpose(0, 2, 1, 3)
    fk = f_cum.reshape(n, nq, tq, kvh, gq).transpose(0, 3, 1, 4, 2)
    return pl.pallas_call(
        functools.partial(_fox_attn_kernel, gq=gq, tq=tq, scale=hd ** -0.5),
        grid=(n, heads, nq),
        in_specs=[pl.BlockSpec((1, tq, hd), lambda b, h, i: (b, i, h)),
                  pl.BlockSpec((1, t, hd), lambda b, h, i: (b, 0, h // gq)),
                  pl.BlockSpec((1, t, hd), lambda b, h, i: (b, 0, kvh + h // gq)),
                  pl.BlockSpec((1, 1, tq, gq), lambda b, h, i: (b, h // gq, i, 0)),
                  pl.BlockSpec((1, 1, nq, gq, tq), lambda b, h, i: (b, h // gq, 0, 0, 0))],
        out_specs=pl.BlockSpec((1, tq, hd), lambda b, h, i: (b, i, h)),
        out_shape=jax.ShapeDtypeStruct((n, t, qd), BF16),
        compiler_params=_cp("parallel", "parallel", "arbitrary"),
        name="fox_attn",
    )(q, kvb, kvb, fq, fk)


def _mla_attn_kernel(qn_ref, qr_ref, kn_ref, kr_ref, v_ref, o_ref, *, tq, scale):
    qi = pl.program_id(2)
    qn = qn_ref[0]
    qr = qr_ref[0, 0]
    rowpos = qi * tq + lax.broadcasted_iota(jnp.int32, (tq, 1), 0)

    def kv_step(j, carry, diagonal):
        m_prev, l_prev, acc = carry
        start = pl.multiple_of(j * tq, tq)
        s = lax.dot_general(qn, kn_ref[0, pl.ds(start, tq), :], NT, preferred_element_type=F32)
        s = (s + lax.dot_general(qr, kr_ref[0, pl.ds(start, tq), :], NT, preferred_element_type=F32)) * scale2
        if diagonal:
            s = jnp.where(rowpos >= (j * tq + lax.broadcasted_iota(jnp.int32, (1, tq), 1)), s, -jnp.inf)
        m_new, corr, p, l_new = _softmax_update2(s, m_prev, l_prev)
        acc = acc * corr + jnp.dot(p.astype(BF16), v_ref[0, pl.ds(start, tq), :], preferred_element_type=F32)
        return m_new, l_new, acc

    scale2 = scale * LOG2E
    init = (jnp.full((tq, 1), -jnp.inf, F32), jnp.zeros((tq, 1), F32), jnp.zeros((tq, v_ref.shape[2]), F32))
    _, l, acc = _causal_blocks(qi, kv_step, init)
    o_ref[0] = (acc / l).astype(o_ref.dtype)


def _mla_prompt_attention(qn, qr, kvb, kr, heads, dn, dr, dv):
    n, t, _ = qn.shape
    tq = _pick(t, (512, 256, 128))
    nq = t // tq
    return pl.pallas_call(
        functools.partial(_mla_attn_kernel, tq=tq, scale=(dn + dr) ** -0.5),
        grid=(n, heads, nq),
        in_specs=[pl.BlockSpec((1, tq, dn), lambda b, h, i: (b, i, h)),
                  pl.BlockSpec((1, 1, tq, dr), lambda b, h, i: (b, h, i, 0)),
                  pl.BlockSpec((1, t, dn), lambda b, h, i: (b, 0, 2 * h)),
                  pl.BlockSpec((1, t, dr), lambda b, h, i: (b, 0, 0)),
                  pl.BlockSpec((1, t, dv), lambda b, h, i: (b, 0, 2 * h + 1))],
        out_specs=pl.BlockSpec((1, tq, dv), lambda b, h, i: (b, i, h)),
        out_shape=jax.ShapeDtypeStruct((n, t, heads * dv), BF16),
        compiler_params=_cp("parallel", "parallel", "arbitrary"),
        name="mla_attn",
    )(qn, qr, kvb, kr, kvb)


def _fox_decode_kernel(pt_ref, *refs, pp, kvh, gq, hd, scale):
    del pt_ref
    q_ref, knew_ref, vnew_ref, cnew_ref = refs[:4]
    k_refs = refs[4:4 + pp]
    v_refs = refs[4 + pp:4 + 2 * pp]
    f_refs = refs[4 + 2 * pp:4 + 3 * pp]
    o_ref = refs[4 + 3 * pp]
    m_ref, l_ref, acc_ref, carry_ref = refs[5 + 3 * pp:]
    step = pl.program_id(1)
    ps = f_refs[0].shape[2]

    @pl.when(step == 0)
    def _():
        m_ref[...] = jnp.full(m_ref.shape, -jnp.inf, F32)
        l_ref[...] = jnp.zeros(l_ref.shape, F32)
        acc_ref[...] = jnp.zeros(acc_ref.shape, F32)
        carry_ref[...] = jnp.zeros(carry_ref.shape, F32)

    q = q_ref[0]
    cnew = cnew_ref[0]
    later = (lax.broadcasted_iota(jnp.int32, (ps, ps), 0) > lax.broadcasted_iota(jnp.int32, (ps, ps), 1)).astype(F32)
    carry = carry_ref[...]
    hh = kvh * gq
    own = (lax.shift_right_logical(lax.broadcasted_iota(jnp.int32, (hh, kvh * ps), 0), _log2(gq))
           == lax.shift_right_logical(lax.broadcasted_iota(jnp.int32, (hh, kvh * ps), 1), _log2(ps)))

    def by_head(ref):
        return jnp.concatenate([ref[0, pl.ds(kk, ps, stride=kvh), :] for kk in range(kvh)], axis=0).astype(BF16)

    scores = []
    for i in range(pp):
        lf = f_refs[i][0]
        after = jnp.dot(lf, later, preferred_element_type=F32, precision=HI) + carry
        carry = carry + jnp.sum(lf, axis=1, keepdims=True)
        bias = cnew + after
        s = lax.dot_general(q, by_head(k_refs[i]), NT, preferred_element_type=F32) * scale
        scores.append(jnp.where(own, s + jnp.concatenate([bias] * kvh, axis=1), -jnp.inf))
    carry_ref[...] = carry
    s_all = jnp.concatenate(scores, axis=1)
    m_new, corr, p_all, l_new = _softmax_update(s_all, m_ref[...], l_ref[...])
    m_ref[...] = m_new
    l_ref[...] = l_new
    pb = p_all.astype(BF16)
    acc = acc_ref[...] * corr
    width = kvh * ps
    for i in range(pp):
        acc = acc + jnp.dot(pb[:, i * width:(i + 1) * width], by_head(v_refs[i]), preferred_element_type=F32)
    acc_ref[...] = acc

    @pl.when(step == pl.num_programs(1) - 1)
    def _():
        qf = q.astype(F32)
        kn = knew_ref[0].astype(BF16).astype(F32)
        vn = vnew_ref[0].astype(BF16).astype(F32)
        kx = jnp.concatenate([jnp.broadcast_to(kn[kk:kk + 1], (gq, hd)) for kk in range(kvh)], axis=0)
        vx = jnp.concatenate([jnp.broadcast_to(vn[kk:kk + 1], (gq, hd)) for kk in range(kvh)], axis=0)
        s_new = jnp.sum(qf * kx, axis=1, keepdims=True) * scale + (cnew - cnew)
        m_fin, corr_fin, p_new, l_fin = _softmax_update(s_new, m_ref[...], l_ref[...])
        acc = acc_ref[...] * corr_fin + p_new.astype(BF16).astype(F32) * vx
        o_ref[0] = acc / l_fin


def _fox_sample_attention(q, k_new, v_new, logf_new, cache_k, cache_v, cache_logf, layer, page_table):
    n = q.shape[0]
    _, pool, ps, kvh, hd = cache_k.shape
    hh = cache_logf.shape[-1]
    gq = hh // kvh
    npg = page_table.shape[1]
    pp = _pick(npg, (16, 8, 4, 2, 1))
    ck = cache_k.reshape(-1, ps * kvh, hd)
    cv = cache_v.reshape(-1, ps * kvh, hd)
    cf = cache_logf.reshape(-1, ps, hh).transpose(0, 2, 1)
    base = layer * pool

    def page_map(i):
        return lambda s, p, pt: (base + pt[s, npg - 1 - (p * pp + i)], 0, 0)

    seq3 = lambda s, p, pt: (s, 0, 0)
    in_specs = [pl.BlockSpec((1, hh, hd), seq3),
                pl.BlockSpec((1, kvh, hd), seq3),
                pl.BlockSpec((1, kvh, hd), seq3),
                pl.BlockSpec((1, hh, 1), seq3)]
    in_specs += [pl.BlockSpec((1, ps * kvh, hd), page_map(i)) for i in range(pp)]
    in_specs += [pl.BlockSpec((1, ps * kvh, hd), page_map(i)) for i in range(pp)]
    in_specs += [pl.BlockSpec((1, hh, ps), page_map(i)) for i in range(pp)]
    out = pl.pallas_call(
        functools.partial(_fox_decode_kernel, pp=pp, kvh=kvh, gq=gq, hd=hd, scale=hd ** -0.5),
        grid_spec=pltpu.PrefetchScalarGridSpec(
            num_scalar_prefetch=1,
            grid=(n, npg // pp),
            in_specs=in_specs,
            out_specs=pl.BlockSpec((1, hh, hd), seq3),
            scratch_shapes=[pltpu.VMEM((hh, 1), F32), pltpu.VMEM((hh, 1), F32),
                            pltpu.VMEM((hh, hd), F32), pltpu.VMEM((hh, 1), F32)]),
        out_shape=jax.ShapeDtypeStruct((n, hh, hd), F32),
        compiler_params=_cp("parallel", "arbitrary"),
        name="fox_decode",
    )(page_table, q.reshape(n, hh, hd).astype(BF16), k_new.reshape(n, kvh, hd), v_new.reshape(n, kvh, hd),
      logf_new.reshape(n, hh, 1), *([ck] * pp), *([cv] * pp), *([cf] * pp))
    return out.reshape(n, hh * hd).astype(BF16)


def _mla_decode_kernel(pt_ref, *refs, pp, scale):
    del pt_ref
    ql_ref, qr_ref, cnew_ref, rnew_ref = refs[:4]
    c_refs = refs[4:4 + pp]
    r_refs = refs[4 + pp:4 + 2 * pp]
    o_ref = refs[4 + 2 * pp]
    m_ref, l_ref, acc_ref = refs[5 + 2 * pp:]
    step = pl.program_id(1)
    ps = c_refs[0].shape[1]

    @pl.when(step == 0)
    def _():
        m_ref[...] = jnp.full(m_ref.shape, -jnp.inf, F32)
        l_ref[...] = jnp.zeros(l_ref.shape, F32)
        acc_ref[...] = jnp.zeros(acc_ref.shape, F32)

    ql = ql_ref[0]
    qr = qr_ref[0]
    cbs, scores = [], []
    for i in range(pp):
        cb = c_refs[i][0].astype(BF16)
        rb = r_refs[i][0].astype(BF16)
        s = lax.dot_general(ql, cb, NT, preferred_element_type=F32)
        scores.append((s + lax.dot_general(qr, rb, NT, preferred_element_type=F32)) * scale)
        cbs.append(cb)
    s_all = jnp.concatenate(scores, axis=1)
    m_new, corr, p_all, l_new = _softmax_update(s_all, m_ref[...], l_ref[...])
    m_ref[...] = m_new
    l_ref[...] = l_new
    pb = p_all.astype(BF16)
    acc = acc_ref[...] * corr
    for i in range(pp):
        acc = acc + jnp.dot(pb[:, i * ps:(i + 1) * ps], cbs[i], preferred_element_type=F32)
    acc_ref[...] = acc

    @pl.when(step == pl.num_programs(1) - 1)
    def _():
        cn = cnew_ref[0].astype(BF16).astype(F32)
        rn = rnew_ref[0].astype(BF16).astype(F32)
        s_new = (jnp.sum(ql.astype(F32) * cn, axis=1, keepdims=True)
                 + jnp.sum(qr.astype(F32) * rn, axis=1, keepdims=True)) * scale
        m_fin, corr_fin, p_new, l_fin = _softmax_update(s_new, m_ref[...], l_ref[...])
        acc_fin = acc_ref[...] * corr_fin + p_new.astype(BF16).astype(F32) * cn
        o_ref[0] = (acc_fin / l_fin).astype(o_ref.dtype)


def _mla_sample_attention(q_lat, q_rot, ckv_new, kpe_new, cache_ckv, cache_kpe, layer, page_table, heads, scale):
    n = q_lat.shape[0]
    _, pool, ps, kl = cache_ckv.shape
    rr = cache_kpe.shape[-1]
    npg = page_table.shape[1]
    pp = _pick(npg, (16, 8, 4, 2, 1))
    cc = cache_ckv.reshape(-1, ps, kl)
    cr = cache_kpe.reshape(-1, ps, rr)
    base = layer * pool

    def page_map(i):
        return lambda s, p, pt: (base + pt[s, p * pp + i], 0, 0)

    seq3 = lambda s, p, pt: (s, 0, 0)
    in_specs = [pl.BlockSpec((1, heads, kl), seq3), pl.BlockSpec((1, heads, rr), seq3),
                pl.BlockSpec((1, 1, kl), seq3), pl.BlockSpec((1, 1, rr), seq3)]
    in_specs += [pl.BlockSpec((1, ps, kl), page_map(i)) for i in range(pp)]
    in_specs += [pl.BlockSpec((1, ps, rr), page_map(i)) for i in range(pp)]
    out = pl.pallas_call(
        functools.partial(_mla_decode_kernel, pp=pp, scale=scale),
        grid_spec=pltpu.PrefetchScalarGridSpec(
            num_scalar_prefetch=1,
            grid=(n, npg // pp),
            in_specs=in_specs,
            out_specs=pl.BlockSpec((1, heads, kl), seq3),
            scratch_shapes=[pltpu.VMEM((heads, 1), F32), pltpu.VMEM((heads, 1), F32),
                            pltpu.VMEM((heads, kl), F32)]),
        out_shape=jax.ShapeDtypeStruct((n, heads, kl), BF16),
        compiler_params=_cp("parallel", "arbitrary"),
        name="mla_decode",
    )(page_table, q_lat.reshape(n, heads, kl), q_rot.reshape(n, heads, rr),
      ckv_new.reshape(n, 1, kl), kpe_new.reshape(n, 1, rr), *([cc] * pp), *([cr] * pp))
    return out.reshape(n, heads * kl)


def _mla_norm_kernel(z_ref, gq_ref, gkv_ref, cq_ref, ckv_ref, ckvb_ref, *, ql):
    z = z_ref[...]
    zq, zk = z[:, :ql], z[:, ql:]
    cq = zq * lax.rsqrt(jnp.mean(zq * zq, axis=-1, keepdims=True) + RMS_EPS) * gq_ref[...]
    ck = zk * lax.rsqrt(jnp.mean(zk * zk, axis=-1, keepdims=True) + RMS_EPS) * gkv_ref[...]
    cq_ref[...] = cq.astype(BF16)
    ckv_ref[...] = ck
    ckvb_ref[...] = ck.astype(BF16)


def _mla_norm(z, g_q, g_kv):
    m, w = z.shape
    ql = g_q.shape[0]
    kl = g_kv.shape[0]
    tm = _pick(m, (512, 256, 128))
    return pl.pallas_call(
        functools.partial(_mla_norm_kernel, ql=ql),
        grid=(m // tm,),
        in_specs=[pl.BlockSpec((tm, w), lambda i: (i, 0)),
                  pl.BlockSpec((1, ql), lambda i: (0, 0)),
                  pl.BlockSpec((1, kl), lambda i: (0, 0))],
        out_specs=[pl.BlockSpec((tm, ql), lambda i: (i, 0)),
                   pl.BlockSpec((tm, kl), lambda i: (i, 0)),
                   pl.BlockSpec((tm, kl), lambda i: (i, 0))],
        out_shape=[jax.ShapeDtypeStruct((m, ql), BF16), jax.ShapeDtypeStruct((m, kl), F32),
                   jax.ShapeDtypeStruct((m, kl), BF16)],
        compiler_params=_cp("parallel"),
        name="mla_norm",
    )(z, g_q.astype(F32).reshape(1, ql), g_kv.astype(F32).reshape(1, kl))


def _rope_tables(pos, half, reps):
    inv = ROPE_BASE ** (-jnp.arange(half, dtype=F32) / half)
    ang = pos[:, None] * inv
    cos, sin = jnp.cos(ang), jnp.sin(ang)
    return (jnp.tile(jnp.concatenate([cos, cos], axis=-1), (1, reps)),
            jnp.tile(jnp.concatenate([-sin, sin], axis=-1), (1, reps)))


def _swap_halves(w, width):
    k, n = w.shape
    return w.reshape(k, n // width, 2, width // 2)[:, :, ::-1, :].reshape(k, n)


def _sort_desc(v):
    v = list(v)
    n = len(v)
    k = 2
    while k <= n:
        j = k // 2
        while j >= 1:
            for i in range(n):
                l = i ^ j
                if l > i:
                    hi, lo = jnp.maximum(v[i], v[l]), jnp.minimum(v[i], v[l])
                    v[i], v[l] = (hi, lo) if (i & k) == 0 else (lo, hi)
            j //= 2
        k *= 2
    return v


def _merge_desc(v):
    v = list(v)
    j = len(v) // 2
    while j >= 1:
        for i in range(len(v)):
            l = i ^ j
            if l > i:
                v[i], v[l] = jnp.maximum(v[i], v[l]), jnp.minimum(v[i], v[l])
        j //= 2
    return v


def _top_of_two(a, b):
    n = len(a)
    return _merge_desc([jnp.maximum(a[i], b[n - 1 - i]) for i in range(n)])


def _fold_sublanes(v):
    shift = v[0].shape[0] // 2
    while shift >= 1:
        v = _top_of_two(v, [pltpu.roll(x, shift, 0) for x in v])
        shift //= 2
    return v


def _peer_route_kernel(qp_ref, sk_ref, s1_ref, c1_ref, s2_ref, e2_ref, tau_ref, *, heads, nkeys, topk):
    tt = qp_ref.shape[0]
    sub = lax.broadcasted_iota(jnp.int32, (SUBLANE, tt), 0)
    for h in range(heads):
        sts, vals = [], []
        for c in range(2):
            qh = qp_ref[:, (2 * h + c) * nkeys:(2 * h + c + 1) * nkeys]
            st = lax.dot_general(sk_ref[c, h], qh, NT, preferred_element_type=F32, precision=HI)
            sts.append(st)
            slabs = [st[SUBLANE * i:SUBLANE * (i + 1), :] for i in range(nkeys // SUBLANE)]
            vals.append(_fold_sublanes(_sort_desc(slabs)))
        v1, v2 = vals
        cands = []
        for half in range(topk // SUBLANE):
            b = v2[half * SUBLANE + SUBLANE - 1]
            for s in range(SUBLANE - 2, -1, -1):
                b = jnp.where(sub == s, v2[half * SUBLANE + s], b)
            cands.append([a + b for a in v1])
        top = cands[0]
        for other in cands[1:]:
            top = _top_of_two(top, other)
        top = _fold_sublanes(top)
        mx = top[0][0:1]
        z = jnp.zeros_like(mx)
        for r in range(topk):
            z = z + jnp.exp(top[r][0:1] - mx)
        s1_ref[h] = sts[0]
        s2_ref[h] = sts[1]
        c1_ref[h] = jnp.exp(sts[0] - v1[0][0:1]) / z
        e2_ref[h] = jnp.exp(sts[1] - v2[0][0:1])
        tau_ref[h:h + 1, :] = top[topk - 1][0:1]


def _peer_dense_kernel(h_ref, u_ref, v_ref, s1_ref, c1_ref, s2_ref, e2_ref, tau_ref, o_ref, *rest,
                       heads, nkeys, emit_bf16):
    ht_ref = rest[-1]
    e = pl.program_id(1)
    eb = u_ref.shape[0]
    tt = h_ref.shape[0]
    per = eb // nkeys

    @pl.when(e == 0)
    def _():
        ht_ref[...] = h_ref[...].astype(F32).T.astype(BF16)
        o_ref[...] = jnp.zeros(o_ref.shape, F32)

    u, v = u_ref[...].astype(BF16), v_ref[...].astype(BF16)
    if emit_bf16:
        rest[0][...] = u
        rest[1][...] = v
    st = jnp.dot(u, ht_ref[...], preferred_element_type=F32)
    act = jax.nn.gelu(st)
    parts = []
    for al in range(per):
        a = e * per + al
        w = jnp.zeros((nkeys, tt), F32)
        for h in range(heads):
            sc = s2_ref[h] + s1_ref[h, pl.ds(a, 1), :]
            w = w + jnp.where(sc >= tau_ref[h:h + 1, :], e2_ref[h] * c1_ref[h, pl.ds(a, 1), :], 0.0)
        parts.append(w * act[al * nkeys:(al + 1) * nkeys])
    pt = parts[0] if len(parts) == 1 else jnp.concatenate(parts, axis=0)
    o_ref[...] += jnp.dot(pt.T.astype(BF16), v, preferred_element_type=F32)


def _peer_ffn(h, w_q, subkeys, u_tab, v_tab, layer, emit_bf16=False):
    m, d = h.shape
    _, heads, nkeys, kd2 = subkeys.shape
    assert nkeys == LANE and kd2 == LANE, "product-key halves are laid out on one 128-lane tile"
    assert nkeys // SUBLANE == PEER_TOPK, "the top-k network keeps one value per slab of keys"
    ne = u_tab.shape[-2]
    qp = _mm(h, w_q, layer=layer)
    tt = _pick(m, (256, 128))
    tab = jax.ShapeDtypeStruct((heads, nkeys, m), F32)
    tspec = pl.BlockSpec((heads, nkeys, tt), lambda i: (0, 0, i))
    s1, c1, s2, e2, tau = pl.pallas_call(
        functools.partial(_peer_route_kernel, heads=heads, nkeys=nkeys, topk=PEER_TOPK),
        grid=(m // tt,),
        in_specs=[pl.BlockSpec((tt, qp.shape[1]), lambda i: (i, 0)),
                  pl.BlockSpec((2, heads, nkeys, kd2), lambda i: (0, 0, 0, 0))],
        out_specs=[tspec, tspec, tspec, tspec, pl.BlockSpec((heads, tt), lambda i: (0, i))],
        out_shape=[tab, tab, tab, tab, jax.ShapeDtypeStruct((heads, m), F32)],
        compiler_params=_cp("parallel"),
        name="peer_route",
    )(qp, subkeys.astype(F32))
    td = _pick(m, (512, 256, 128))
    eb = _pick(ne, (256, 128)) if emit_bf16 else _pick(ne, (512, 256, 128))
    once = pl.Buffered(1)
    dspec = pl.BlockSpec((heads, nkeys, td), lambda i, e: (0, 0, i), pipeline_mode=once)
    ospec = pl.BlockSpec((td, d), lambda i, e: (i, 0))
    oshape = jax.ShapeDtypeStruct((m, d), F32)
    if emit_bf16:
        assert m == td, "every expert block is cast and written exactly once"
        wspec = pl.BlockSpec((None, eb, d), lambda i, e: (layer, e, 0))
        bspec = pl.BlockSpec((eb, d), lambda i, e: (e, 0))
        bshape = jax.ShapeDtypeStruct((ne, d), BF16)
        ospec, oshape = [ospec, bspec, bspec], [oshape, bshape, bshape]
    else:
        wspec = pl.BlockSpec((eb, d), lambda i, e: (e, 0))
    return pl.pallas_call(
        functools.partial(_peer_dense_kernel, heads=heads, nkeys=nkeys, emit_bf16=emit_bf16),
        grid=(m // td, ne // eb),
        in_specs=[pl.BlockSpec((td, d), lambda i, e: (i, 0), pipeline_mode=once),
                  wspec, wspec, dspec, dspec, dspec, dspec,
                  pl.BlockSpec((heads, td), lambda i, e: (0, i), pipeline_mode=once)],
        out_specs=ospec,
        out_shape=oshape,
        scratch_shapes=[pltpu.VMEM((d, td), BF16)],
        compiler_params=_cp("parallel", "arbitrary"),
        name="peer_dense",
    )(h, u_tab, v_tab, s1, c1, s2, e2, tau)


def kernel(x_prompt, x_sample, state_s5_re, state_s5_im, cache_fox_k, cache_fox_v, cache_fox_logf, cache_mla_ckv, cache_mla_kpe, page_table, c_prompt, c_sample, ada_w, ada_b, ln_g, ln_b, s5_w_in, s5_lam_re, s5_lam_im, s5_log_step, s5_b_re, s5_b_im, s5_c_re, s5_c_im, s5_d, s5_w_glu, fox_w_in, fox_b_f, fox_w_o, mla_w_in, mla_q_norm, mla_kv_norm, mla_w_uq, mla_w_ukv, mla_w_o, peer_w_q, peer_subkeys, peer_u, peer_v):
    n_p, t_p, dm = x_prompt.shape
    n_s, t_s, _ = x_sample.shape
    assert t_s == 1, "the sample group decodes one token per sequence"
    depth = ada_w.shape[0]
    alpha = (2 * depth) ** 0.25
    past_len = page_table.shape[1] * cache_fox_k.shape[2]
    fox_heads = fox_b_f.shape[-1]
    fox_kvh, fox_hd = cache_fox_k.shape[3], cache_fox_k.shape[4]
    mla_ql, mla_kl, mla_r = mla_q_norm.shape[-1], mla_kv_norm.shape[-1], cache_mla_kpe.shape[-1]
    mla_heads = (mla_w_uq.shape[-1] - mla_w_ukv.shape[-1] + mla_w_o.shape[-2]) // mla_r
    mla_dv = mla_w_o.shape[-2] // mla_heads
    mla_dn = mla_w_ukv.shape[-1] // mla_heads - mla_dv
    assert mla_dn == mla_dv, "k_nope and v column blocks of w_ukv are addressed with one block width"

    s5_w_in_b, s5_w_glu_b = s5_w_in.astype(BF16), s5_w_glu.astype(BF16)
    fox_w_in_b, fox_w_o_b = fox_w_in.astype(BF16), fox_w_o.astype(BF16)
    mla_w_in_b, mla_w_uq_b = mla_w_in.astype(BF16), mla_w_uq.astype(BF16)
    mla_w_ukv_b, mla_w_o_b = mla_w_ukv.astype(BF16), mla_w_o.astype(BF16)
    peer_w_q_b = peer_w_q.astype(BF16)

    rows = n_p + n_s
    rpad = -(-rows // SUBLANE) * SUBLANE
    c_all = jnp.pad(jnp.concatenate([c_prompt, c_sample], axis=0).astype(F32), ((0, rpad - rows), (0, 0)))
    mod = _modulation_all(c_all, ada_w.astype(F32), ada_b.astype(F32))

    def mods(i, s):
        m = mod[2 * i + s]
        mp = m[:n_p, None, :]
        ms = m[n_p:rows][None]
        cut = lambda a: (a[..., :dm], a[..., dm:2 * dm], a[..., 2 * dm:])
        return cut(mp), cut(ms)

    pos_p = jnp.arange(t_p, dtype=F32)
    pos_s = past_len + jnp.arange(t_s, dtype=F32)
    cos_p, sin_p = _rope_tables(pos_p, mla_r // 2, LANE // mla_r)
    cos_p, sin_p = jnp.tile(cos_p, (n_p, 1)), jnp.tile(sin_p, (n_p, 1))
    cos_s, sin_s = _rope_tables(jnp.tile(pos_s, (n_s,)), mla_r // 2, LANE // mla_r)

    yp, ys = x_prompt.astype(F32), x_sample.astype(F32).reshape(1, n_s, dm)
    (shp, scp, _), (shs, scs, _) = mods(0, 0)
    hp = _ln_mod(yp, scale=scp, shift=shp, alpha=alpha)
    hs = _ln_mod(ys, scale=scs, shift=shs, alpha=alpha)

    out = {k: [] for k in ("s5r_p", "s5i_p", "s5r_s", "s5i_s", "fk_p", "fv_p", "fl_p", "fk_s", "fv_s", "fl_s",
                           "ckv_p", "kpe_p", "ckv_s", "kpe_s")}
    for i in range(depth):
        kind, j = i % 3, i // 3
        (_, _, gate_p), (_, _, gate_s) = mods(i, 0)
        if kind == 0:
            prm = (s5_lam_re[j], s5_lam_im[j], s5_log_step[j], s5_b_re[j], s5_b_im[j], s5_c_re[j], s5_c_im[j])
            op, sr, si = _s5_prompt(hp, prm, s5_w_in_b, s5_d[j], s5_w_glu_b, j)
            out["s5r_p"].append(sr)
            out["s5i_p"].append(si)
            os_, sr, si = _s5_sample(hs.reshape(n_s, 1, dm), state_s5_re[j], state_s5_im[j], prm, s5_w_in_b,
                                     s5_d[j], s5_w_glu_b, j)
            out["s5r_s"].append(sr)
            out["s5i_s"].append(si)
            os_ = os_.reshape(1, n_s, dm)
        elif kind == 1:
            nq, nkv = fox_heads * fox_hd, fox_kvh * fox_hd
            w_f = fox_w_in_b[j][:, nq + 2 * nkv:]
            h2 = hp.reshape(n_p * t_p, dm)
            q = _mm(h2, fox_w_in_b, (BF16,), layer=j, col0=0, ncols=nq)
            kv, kvb = _mm(h2, fox_w_in_b, (F32, BF16), layer=j, col0=nq, ncols=2 * nkv)
            lf = _mm_logsig(h2, w_f, fox_b_f[j])
            f_cum = _cumsum_time(lf.reshape(n_p, t_p, fox_heads))
            o = _fox_prompt_attention(q.reshape(n_p, t_p, nq), kvb.reshape(n_p, t_p, 2 * nkv), f_cum,
                                      fox_kvh, fox_hd)
            op = _mm(o.reshape(n_p * t_p, nq), fox_w_o_b, layer=j).reshape(n_p, t_p, dm)
            out["fk_p"].append(kv[:, :nkv].reshape(n_p, t_p, fox_kvh, fox_hd))
            out["fv_p"].append(kv[:, nkv:].reshape(n_p, t_p, fox_kvh, fox_hd))
            out["fl_p"].append(lf.reshape(n_p, t_p, fox_heads))
            h2 = hs.reshape(n_s, dm)
            q = _mm(h2, fox_w_in_b, layer=j, col0=0, ncols=nq)
            kv = _mm(h2, fox_w_in_b, layer=j, col0=nq, ncols=2 * nkv)
            lf = _mm_logsig(h2, w_f, fox_b_f[j])
            o = _fox_sample_attention(q, kv[:, :nkv], kv[:, nkv:], lf, cache_fox_k, cache_fox_v,
                                      cache_fox_logf, j, page_table)
            os_ = _mm(o, fox_w_o_b, layer=j).reshape(1, n_s, dm)
            out["fk_s"].append(kv[:, :nkv].reshape(n_s, 1, fox_kvh, fox_hd))
            out["fv_s"].append(kv[:, nkv:].reshape(n_s, 1, fox_kvh, fox_hd))
            out["fl_s"].append(lf.reshape(n_s, 1, fox_heads))
        else:
            nlat = mla_ql + mla_kl
            w_pe = mla_w_in_b[j][:, nlat:]
            w_pe_rot = _swap_halves(w_pe, mla_r)
            wq = mla_w_uq_b[j].reshape(mla_ql, mla_heads, mla_dn + mla_r)
            wq_n = wq[:, :, :mla_dn].reshape(mla_ql, mla_heads * mla_dn)
            wq_r = wq[:, :, mla_dn:].reshape(mla_ql, mla_heads * mla_r)
            wq_r_rot = _swap_halves(wq_r, mla_r)

            def project(h2, cos, sin):
                z = _mm(h2, mla_w_in_b, layer=j, col0=0, ncols=nlat)
                cq, ckv, ckvb = _mla_norm(z, mla_q_norm[j], mla_kv_norm[j])
                kpe = _mm_rope(h2, w_pe, w_pe_rot, cos[:, :mla_r], sin[:, :mla_r], (F32,))
                qn = _mm(cq, wq_n, (BF16,))
                qr = _mm_rope(cq, wq_r, wq_r_rot, cos, sin, (BF16,))
                return qn, qr, ckv, ckvb, kpe

            qn, qr, ckv, ckvb, kpe = project(hp.reshape(n_p * t_p, dm), cos_p, sin_p)
            kvb = _mm(ckvb, mla_w_ukv_b, (BF16,), layer=j)
            qr4 = qr.reshape(n_p, t_p, mla_heads, mla_r).transpose(0, 2, 1, 3)
            o = _mla_prompt_attention(qn.reshape(n_p, t_p, -1), qr4, kvb.reshape(n_p, t_p, -1),
                                      kpe.astype(BF16).reshape(n_p, t_p, mla_r), mla_heads, mla_dn, mla_r, mla_dv)
            op = _mm(o.reshape(n_p * t_p, -1), mla_w_o_b, layer=j).reshape(n_p, t_p, dm)
            out["ckv_p"].append(ckv.reshape(n_p, t_p, mla_kl))
            out["kpe_p"].append(kpe.reshape(n_p, t_p, mla_r))
            qn, qr, ckv, ckvb, kpe = project(hs.reshape(n_s, dm), cos_s, sin_s)
            q_lat = _head_mm(qn, mla_w_ukv_b, j, mla_heads, 0, True, BF16)
            a = _mla_sample_attention(q_lat, qr, ckv, kpe, cache_mla_ckv, cache_mla_kpe, j, page_table, mla_heads,
                                      (mla_dn + mla_r) ** -0.5)
            o = _head_mm(a, mla_w_ukv_b, j, mla_heads, 1, False, BF16)
            os_ = _mm(o, mla_w_o_b, layer=j).reshape(1, n_s, dm)
            out["ckv_s"].append(ckv.reshape(n_s, 1, mla_kl))
            out["kpe_s"].append(kpe.reshape(n_s, 1, mla_r))
        (shp, scp, gate2_p), (shs, scs, gate2_s) = mods(i, 1)
        yp, hp = _ln_mod(yp, op, gate_p, ln_g[i, 0], ln_b[i, 0], scp, shp, alpha=alpha)
        ys, hs = _ln_mod(ys, os_, gate_s, ln_g[i, 0], ln_b[i, 0], scs, shs, alpha=alpha)
        fs, u_b, v_b = _peer_ffn(hs.reshape(n_s, dm), peer_w_q_b, peer_subkeys[i], peer_u.astype(F32),
                                 peer_v.astype(F32), i, emit_bf16=True)
        fp = _peer_ffn(hp.reshape(n_p * t_p, dm), peer_w_q_b, peer_subkeys[i], u_b, v_b, i)
        fp, fs = fp.reshape(n_p, t_p, dm), fs.reshape(1, n_s, dm)
        if i + 1 < depth:
            (shp, scp, _), (shs, scs, _) = mods(i + 1, 0)
            yp, hp = _ln_mod(yp, fp, gate2_p, ln_g[i, 1], ln_b[i, 1], scp, shp, alpha=alpha)
            ys, hs = _ln_mod(ys, fs, gate2_s, ln_g[i, 1], ln_b[i, 1], scs, shs, alpha=alpha)
        else:
            yp = _ln_mod(yp, fp, gate2_p, ln_g[i, 1], ln_b[i, 1], alpha=alpha)
            ys = _ln_mod(ys, fs, gate2_s, ln_g[i, 1], ln_b[i, 1], alpha=alpha)

    st = lambda k: jnp.stack(out[k])
    return (yp, ys.reshape(n_s, 1, dm),
            st("s5r_p"), st("s5i_p"), st("fk_p"), st("fv_p"), st("fl_p"), st("ckv_p"), st("kpe_p"),
            st("s5r_s"), st("s5i_s"), st("fk_s"), st("fv_s"), st("fl_s"), st("ckv_s"), st("kpe_s"))
```

```python
import functools
import math

import jax
import jax.numpy as jnp
from jax import lax
from jax.experimental import pallas as pl
from jax.experimental.pallas import tpu as pltpu

F32 = jnp.float32
BF16 = jnp.bfloat16
HI = lax.Precision.HIGHEST

LANE = 128
SUBLANE = 8
VMEM_LIMIT = 56 * 1024 * 1024
LN_EPS = 1e-5
RMS_EPS = 1e-6
ROPE_BASE = 10000.0
LOG2E = math.log2(math.e)
S5_CHUNK = 8
PEER_TOPK = 16
NT = (((1,), (1,)), ((), ()))


def _cp(*sem):
    return pltpu.CompilerParams(dimension_semantics=sem, vmem_limit_bytes=VMEM_LIMIT)


def _pick(n, cands):
    for c in cands:
        if c <= n and n % c == 0:
            return c
    return n


def _wspec(w, k, tn, layer, col0, ngrid):
    off = col0 // tn
    assert col0 % tn == 0
    if w.ndim == 3:
        if ngrid == 1:
            return pl.BlockSpec((None, k, tn), lambda j: (layer, 0, off + j))
        return pl.BlockSpec((None, k, tn), lambda i, j: (layer, 0, off + j))
    if ngrid == 1:
        return pl.BlockSpec((k, tn), lambda j: (0, off + j))
    return pl.BlockSpec((k, tn), lambda i, j: (0, off + j))


def _mm_kernel(x_ref, w_ref, *o_refs):
    acc = jnp.dot(x_ref[...], w_ref[...], preferred_element_type=F32)
    for o_ref in o_refs:
        o_ref[...] = acc.astype(o_ref.dtype)


def _mm(x, w, out_dtypes=(F32,), layer=0, col0=0, ncols=None):
    m, k = x.shape
    n = w.shape[-1] - col0 if ncols is None else ncols
    tm = _pick(m, (1024, 512, 256, 128))
    tn = _pick(math.gcd(n, col0) if col0 else n, (1024, 512, 256, 128))
    outs = pl.pallas_call(
        _mm_kernel,
        grid=(m // tm, n // tn),
        in_specs=[pl.BlockSpec((tm, k), lambda i, j: (i, 0)), _wspec(w, k, tn, layer, col0, 2)],
        out_specs=[pl.BlockSpec((tm, tn), lambda i, j: (i, j)) for _ in out_dtypes],
        out_shape=[jax.ShapeDtypeStruct((m, n), d) for d in out_dtypes],
        compiler_params=_cp("parallel", "arbitrary"),
        name="mm",
    )(x, w)
    return outs[0] if len(out_dtypes) == 1 else outs


def _glu_kernel(x_ref, wv_ref, wg_ref, o_ref):
    x = x_ref[...]
    val = jnp.dot(x, wv_ref[...], preferred_element_type=F32)
    gate = jnp.dot(x, wg_ref[...], preferred_element_type=F32)
    o_ref[...] = val * jax.nn.sigmoid(gate)


def _mm_glu(x, w, layer):
    m, k = x.shape
    n = w.shape[-1] // 2
    tm = _pick(m, (1024, 512, 256, 128))
    tn = _pick(n, (512, 256, 128))
    return pl.pallas_call(
        _glu_kernel,
        grid=(m // tm, n // tn),
        in_specs=[pl.BlockSpec((tm, k), lambda i, j: (i, 0)),
                  _wspec(w, k, tn, layer, 0, 2), _wspec(w, k, tn, layer, n, 2)],
        out_specs=pl.BlockSpec((tm, tn), lambda i, j: (i, j)),
        out_shape=jax.ShapeDtypeStruct((m, n), F32),
        compiler_params=_cp("parallel", "arbitrary"),
        name="mm_glu",
    )(x, w, w)


def _logsig_kernel(x_ref, w_ref, b_ref, o_ref):
    z = jnp.dot(x_ref[...], w_ref[...], preferred_element_type=F32) + b_ref[...]
    o_ref[...] = jnp.minimum(z, 0.0) - jnp.log1p(jnp.exp(-jnp.abs(z)))


def _mm_logsig(x, w, b):
    m, k = x.shape
    n = w.shape[1]
    tm = _pick(m, (1024, 512, 256, 128))
    return pl.pallas_call(
        _logsig_kernel,
        grid=(m // tm,),
        in_specs=[pl.BlockSpec((tm, k), lambda i: (i, 0)),
                  pl.BlockSpec((k, n), lambda i: (0, 0)),
                  pl.BlockSpec((1, n), lambda i: (0, 0))],
        out_specs=pl.BlockSpec((tm, n), lambda i: (i, 0)),
        out_shape=jax.ShapeDtypeStruct((m, n), F32),
        compiler_params=_cp("parallel"),
        name="mm_logsig",
    )(x, w, b.reshape(1, n).astype(F32))


def _rope_kernel(x_ref, w_ref, wr_ref, cos_ref, sin_ref, *o_refs):
    x = x_ref[...]
    a = jnp.dot(x, w_ref[...], preferred_element_type=F32)
    b = jnp.dot(x, wr_ref[...], preferred_element_type=F32)
    reps = a.shape[1] // cos_ref.shape[1]
    cos, sin = cos_ref[...], sin_ref[...]
    if reps > 1:
        cos = jnp.concatenate([cos] * reps, axis=1)
        sin = jnp.concatenate([sin] * reps, axis=1)
    out = a * cos + b * sin
    for o_ref in o_refs:
        o_ref[...] = out.astype(o_ref.dtype)


def _mm_rope(x, w, w_rot, cos, sin, out_dtypes):
    m, k = x.shape
    n = w.shape[1]
    period = cos.shape[1]
    tm = _pick(m, (1024, 512, 256, 128))
    tn = _pick(n, (1024, 512, 256, 128)) if n % period == 0 and period % LANE == 0 else n
    outs = pl.pallas_call(
        _rope_kernel,
        grid=(m // tm, n // tn),
        in_specs=[pl.BlockSpec((tm, k), lambda i, j: (i, 0)),
                  pl.BlockSpec((k, tn), lambda i, j: (0, j)),
                  pl.BlockSpec((k, tn), lambda i, j: (0, j)),
                  pl.BlockSpec((tm, period), lambda i, j: (i, 0)),
                  pl.BlockSpec((tm, period), lambda i, j: (i, 0))],
        out_specs=[pl.BlockSpec((tm, tn), lambda i, j: (i, j)) for _ in out_dtypes],
        out_shape=[jax.ShapeDtypeStruct((m, n), d) for d in out_dtypes],
        compiler_params=_cp("parallel", "arbitrary"),
        name="mm_rope",
    )(x, w, w_rot, cos, sin)
    return outs[0] if len(out_dtypes) == 1 else outs


def _headmm_kernel(x_ref, w_ref, o_ref, *, trans_w):
    if trans_w:
        acc = lax.dot_general(x_ref[...], w_ref[...], NT, preferred_element_type=F32)
    else:
        acc = jnp.dot(x_ref[...], w_ref[...], preferred_element_type=F32)
    o_ref[...] = acc.astype(o_ref.dtype)


def _head_mm(x, w, layer, heads, w_off, trans_w, out_dtype):
    m = x.shape[0]
    dx = x.shape[1] // heads
    r = w.shape[1]
    dw = w.shape[2] // (2 * heads)
    do = r if trans_w else dw
    return pl.pallas_call(
        functools.partial(_headmm_kernel, trans_w=trans_w),
        grid=(heads,),
        in_specs=[pl.BlockSpec((m, dx), lambda h: (0, h)),
                  pl.BlockSpec((None, r, dw), lambda h: (layer, 0, 2 * h + w_off))],
        out_specs=pl.BlockSpec((m, do), lambda h: (0, h)),
        out_shape=jax.ShapeDtypeStruct((m, heads * do), out_dtype),
        compiler_params=_cp("parallel"),
        name="head_mm",
    )(x, w)


def _mod_kernel(c_ref, w_ref, b_ref, o_ref):
    c = c_ref[...]
    a = (c * jax.nn.sigmoid(c)).astype(BF16)
    o_ref[0] = jnp.dot(a, w_ref[0].astype(BF16), preferred_element_type=F32) + b_ref[0]


def _modulation_all(c_all, ada_w, ada_b):
    r, d = c_all.shape
    depth = ada_w.shape[0]
    s = depth * 2
    n3 = ada_w.shape[-1]
    w = ada_w.reshape(s, d, n3)
    b = ada_b.reshape(s, 1, n3)
    tn = _pick(n3, (512, 256, 128))
    return pl.pallas_call(
        _mod_kernel,
        grid=(s, n3 // tn),
        in_specs=[pl.BlockSpec((r, d), lambda i, j: (0, 0)),
                  pl.BlockSpec((1, d, tn), lambda i, j: (i, 0, j)),
                  pl.BlockSpec((1, 1, tn), lambda i, j: (i, 0, j))],
        out_specs=pl.BlockSpec((1, r, tn), lambda i, j: (i, 0, j)),
        out_shape=jax.ShapeDtypeStruct((s, r, n3), F32),
        compiler_params=_cp("parallel", "arbitrary"),
        name="modulation",
    )(c_all, w, b)


def _ln_mod_kernel(*refs, alpha, do_ln, do_mod):
    it = iter(refs)
    x_ref = next(it)
    if do_ln:
        f_ref, gate_ref, g_ref, b_ref = next(it), next(it), next(it), next(it)
    if do_mod:
        scale_ref, shift_ref = next(it), next(it)
    if do_ln:
        y_ref = next(it)
    if do_mod:
        h_ref = next(it)
    y = x_ref[0]
    if do_ln:
        xf = alpha * y + (1.0 + gate_ref[0]) * f_ref[0]
        mu = jnp.mean(xf, axis=-1, keepdims=True)
        xc = xf - mu
        var = jnp.mean(xc * xc, axis=-1, keepdims=True)
        y = xc * lax.rsqrt(var + LN_EPS) * g_ref[...] + b_ref[...]
        y_ref[0] = y
    if do_mod:
        h_ref[0] = (y * (1.0 + scale_ref[0]) + shift_ref[0]).astype(BF16)


def _ln_mod(x, f=None, gate=None, g=None, b=None, scale=None, shift=None, *, alpha):
    n, t, d = x.shape
    do_ln, do_mod = f is not None, scale is not None
    tt = _pick(t, (256, 128))
    xspec = pl.BlockSpec((1, tt, d), lambda i, j: (i, j, 0))

    def mspec(a):
        if a.shape[1] == 1:
            return pl.BlockSpec((1, 1, d), lambda i, j: (i, 0, 0))
        return xspec

    vspec = pl.BlockSpec((1, d), lambda i, j: (0, 0))
    args, specs, oshapes, ospecs = [x], [xspec], [], []
    if do_ln:
        args += [f, gate, g.reshape(1, d), b.reshape(1, d)]
        specs += [xspec, mspec(gate), vspec, vspec]
        oshapes.append(jax.ShapeDtypeStruct((n, t, d), F32))
        ospecs.append(xspec)
    if do_mod:
        args += [scale, shift]
        specs += [mspec(scale), mspec(shift)]
        oshapes.append(jax.ShapeDtypeStruct((n, t, d), BF16))
        ospecs.append(xspec)
    outs = pl.pallas_call(
        functools.partial(_ln_mod_kernel, alpha=alpha, do_ln=do_ln, do_mod=do_mod),
        grid=(n, t // tt),
        in_specs=specs, out_specs=ospecs, out_shape=oshapes,
        compiler_params=_cp("parallel", "parallel"),
        name="ln_mod",
    )(*args)
    return outs if len(outs) > 1 else outs[0]


def _s5_discretize(lam_re, lam_im, log_step, b_re, b_im):
    lr, li = lam_re.astype(F32), lam_im.astype(F32)
    dt = jnp.exp(log_step.astype(F32))[:, None]
    mag, ang = jnp.exp(lr * dt), li * dt
    a_re, a_im = mag * jnp.cos(ang), mag * jnp.sin(ang)
    den = lr * lr + li * li
    f_re = ((a_re - 1.0) * lr + a_im * li) / den
    f_im = (a_im * lr - (a_re - 1.0) * li) / den
    br, bi = b_re.astype(F32), b_im.astype(F32)
    bb_re = f_re[..., None] * br - f_im[..., None] * bi
    bb_im = f_re[..., None] * bi + f_im[..., None] * br
    return lr * dt, li * dt, a_re, a_im, bb_re, bb_im


def _s5_powers(ldt_re, ldt_im, ks):
    k = jnp.asarray(ks, F32)[:, None, None]
    mag, ang = jnp.exp(ldt_re * k), ldt_im * k
    return mag * jnp.cos(ang), mag * jnp.sin(ang)


def _block_diag(x, gl):
    *lead, g, a, b = x.shape
    q = g // gl
    xq = x.reshape(*lead, q, gl, a, b)
    eye = jnp.eye(gl, dtype=x.dtype)
    out = xq[..., :, :, :, None, :] * eye[:, None, :, None]
    return out.reshape(*lead, q, gl * a, gl * b)


def _s5_operators(prm, chunk, n_chunks):
    lam_re, lam_im, log_step, b_re, b_im, c_re, c_im = prm
    g, p = lam_re.shape
    c = b_re.shape[-1]
    gl = LANE // c
    q = g // gl
    ldr, ldi, _, _, bb_re, bb_im = _s5_discretize(lam_re, lam_im, log_step, b_re, b_im)
    cr, ci = c_re.astype(F32), c_im.astype(F32)
    ak_re, ak_im = _s5_powers(ldr, ldi, list(range(chunk + 1)))
    m_re = ak_re[..., None] * bb_re - ak_im[..., None] * bb_im
    m_im = ak_re[..., None] * bb_im + ak_im[..., None] * bb_re
    kk = (jnp.einsum("gop,kgpi->kgio", cr, m_re[:chunk], precision=HI)
          - jnp.einsum("gop,kgpi->kgio", ci, m_im[:chunk], precision=HI))
    kc = kk.reshape(chunk, q, gl * c, c).transpose(1, 0, 2, 3)
    rev = jnp.arange(chunk - 1, -1, -1)
    mc = jnp.stack([m_re[rev], m_im[rev]]).transpose(0, 1, 2, 4, 3)
    mc = mc.reshape(2, chunk, q, gl * c, p).transpose(2, 0, 1, 3, 4)
    pr, pi = ak_re[1:], ak_im[1:]
    v_re = cr[None] * pr[:, :, None, :] - ci[None] * pi[:, :, None, :]
    v_im = -cr[None] * pi[:, :, None, :] - ci[None] * pr[:, :, None, :]
    vc = jnp.stack([v_re, v_im]).transpose(0, 1, 2, 4, 3)
    vc = vc.reshape(2, chunk, q, gl * p, c).transpose(2, 0, 1, 3, 4)
    del n_chunks
    exps = [chunk * (i + 1) for i in range(SUBLANE)] + [chunk * (1 << j) for j in range(_log2(SUBLANE))]
    ad_re, ad_im = _s5_powers(ldr, ldi, exps)
    ad = jnp.concatenate([ad_re.reshape(len(exps), q, gl * p), ad_im.reshape(len(exps), q, gl * p)], axis=-1)
    rows = -(-len(exps) // SUBLANE) * SUBLANE
    ad = jnp.pad(ad.transpose(1, 0, 2), ((0, 0), (0, rows - len(exps)), (0, 0)))
    return kc, mc, vc, ad


def _s5_step_operators(prm):
    lam_re, lam_im, log_step, b_re, b_im, c_re, c_im = prm
    g, p = lam_re.shape
    gl = LANE // b_re.shape[-1]
    q = g // gl
    _, _, a_re, a_im, bb_re, bb_im = _s5_discretize(lam_re, lam_im, log_step, b_re, b_im)
    bb = jnp.concatenate([_block_diag(bb_re.transpose(0, 2, 1), gl),
                          _block_diag(bb_im.transpose(0, 2, 1), gl)], axis=-1).astype(BF16)
    cr, ci = c_re.astype(F32), c_im.astype(F32)
    cc = jnp.concatenate([_block_diag(cr.transpose(0, 2, 1), gl),
                          _block_diag(-ci.transpose(0, 2, 1), gl)], axis=1).astype(BF16)
    a = jnp.concatenate([a_re.reshape(q, 1, gl * p), a_im.reshape(q, 1, gl * p)], axis=-1)
    return bb, cc, a


def _log2(x):
    assert x & (x - 1) == 0, "power of two expected"
    return x.bit_length() - 1


def _lane_tiler(src, dst):
    r = lax.broadcasted_iota(jnp.int32, (src, dst), 0)
    c = lax.broadcasted_iota(jnp.int32, (src, dst), 1)
    return ((c & (src - 1)) == r).astype(BF16)


def _same_block(rows, rblk, cols, cblk):
    r = lax.shift_right_logical(lax.broadcasted_iota(jnp.int32, (rows, cols), 0), _log2(rblk))
    c = lax.shift_right_logical(lax.broadcasted_iota(jnp.int32, (rows, cols), 1), _log2(cblk))
    return r == c


def _s5_scan_kernel(u_ref, kc_ref, mc_ref, vc_ref, ad_ref, d_ref, y_ref, sfin_ref, t_scr, w_scr, v_scr, e_scr,
                    *, n, nk, chunk):
    c = kc_ref.shape[-1]
    p = mc_ref.shape[-1]
    sw = (LANE // c) * p
    tile_c, tile_p = _lane_tiler(c, LANE), _lane_tiler(p, sw)
    mask_t = _same_block(LANE, c, LANE, c)
    mask_w = _same_block(LANE, c, sw, p)
    mask_v = _same_block(sw, p, LANE, c)

    def expand(x, tiler, mask):
        return jnp.where(mask, jnp.dot(x.astype(BF16), tiler, preferred_element_type=F32), 0.0).astype(BF16)

    bd = [expand(kc_ref[0, k], tile_c, mask_t) for k in range(chunk)]
    zero_blk = jnp.zeros((LANE, LANE), BF16)
    for ti in range(chunk):
        rows = slice(ti * LANE, (ti + 1) * LANE)
        for to in range(chunk):
            t_scr[rows, to * LANE:(to + 1) * LANE] = bd[to - ti] if to >= ti else zero_blk
        for ri in range(2):
            w_scr[rows, ri * sw:(ri + 1) * sw] = expand(mc_ref[0, ri, ti], tile_p, mask_w)
    for ri in range(2):
        for to in range(chunk):
            v_scr[ri * sw:(ri + 1) * sw, to * LANE:(to + 1) * LANE] = expand(vc_ref[0, ri, to], tile_c, mask_v)

    u = u_ref[0]
    ub = u.astype(BF16)
    x = jnp.dot(ub, w_scr[...], preferred_element_type=F32)
    half = x.shape[1] // 2
    xr, xi = x[:, :half], x[:, half:]
    nrows = x.shape[0]
    k = lax.rem(lax.broadcasted_iota(jnp.int32, (nrows, 1), 0), nk)
    kin = k & (SUBLANE - 1)
    for j in range(_log2(SUBLANE)):
        d = 1 << j
        ar, ai = ad_ref[0, SUBLANE + j:SUBLANE + j + 1, :half], ad_ref[0, SUBLANE + j:SUBLANE + j + 1, half:]
        keep = kin >= d
        sr = jnp.where(keep, pltpu.roll(xr, d, 0), 0.0)
        si = jnp.where(keep, pltpu.roll(xi, d, 0), 0.0)
        xr, xi = xr + (ar * sr - ai * si), xi + (ar * si + ai * sr)
    e_scr[:, :half] = xr
    e_scr[:, half:] = xi
    pr, pi = ad_ref[0, 0:SUBLANE, :half], ad_ref[0, 0:SUBLANE, half:]
    tiles = nk // SUBLANE
    carries = [None] * n
    for j in range(tiles):
        for s in range(n):
            rows = slice((s * tiles + j) * SUBLANE, (s * tiles + j + 1) * SUBLANE)
            tr, ti = e_scr[rows, :half], e_scr[rows, half:]
            if j > 0:
                cr, ci = carries[s]
                tr, ti = tr + (pr * cr - pi * ci), ti + (pr * ci + pi * cr)
                e_scr[rows, :half] = tr
                e_scr[rows, half:] = ti
            carries[s] = (tr[SUBLANE - 1:SUBLANE], ti[SUBLANE - 1:SUBLANE])
    xr, xi = e_scr[:, :half], e_scr[:, half:]
    keep = k >= 1
    s0 = jnp.concatenate([jnp.where(keep, pltpu.roll(xr, 1, 0), 0.0),
                          jnp.where(keep, pltpu.roll(xi, 1, 0), 0.0)], axis=1).astype(BF16)
    y = jnp.dot(ub, t_scr[...], preferred_element_type=F32)
    y = y + jnp.dot(s0, v_scr[...], preferred_element_type=F32)
    y = y + d_ref[0] * u
    y_ref[0] = jax.nn.gelu(y).astype(BF16)
    for s in range(n):
        last = (s + 1) * nk - 1
        sfin_ref[0, s:s + 1, :] = jnp.concatenate([xr[last:last + 1], xi[last:last + 1]], axis=1)


def _s5_prompt(h, prm, w_in, d_skip, w_glu, layer):
    n, t, dm = h.shape
    g, p = prm[0].shape
    c = prm[3].shape[-1]
    gl = LANE // c
    q = g // gl
    chunk = S5_CHUNK
    nk = t // chunk
    u = _mm(h.reshape(n * t, dm), w_in, layer=layer)
    uc = u.reshape(n, nk, chunk, q, LANE).transpose(3, 0, 1, 2, 4).reshape(q, n * nk, chunk * LANE)
    assert nk % SUBLANE == 0, "the chunk scan works on whole 8-chunk tiles"
    kc, mc, vc, ad = _s5_operators(prm, chunk, nk)
    d_t = jnp.tile(d_skip.astype(F32).reshape(q, 1, LANE), (1, 1, chunk))
    width = chunk * LANE
    sw = 2 * gl * p
    yc, sfin = pl.pallas_call(
        functools.partial(_s5_scan_kernel, n=n, nk=nk, chunk=chunk),
        grid=(q,),
        in_specs=[pl.BlockSpec((1, n * nk, width), lambda i: (i, 0, 0)),
                  pl.BlockSpec((1, chunk, LANE, c), lambda i: (i, 0, 0, 0)),
                  pl.BlockSpec((1, 2, chunk, LANE, p), lambda i: (i, 0, 0, 0, 0)),
                  pl.BlockSpec((1, 2, chunk, gl * p, c), lambda i: (i, 0, 0, 0, 0)),
                  pl.BlockSpec((1, ad.shape[1], sw), lambda i: (i, 0, 0)),
                  pl.BlockSpec((1, 1, width), lambda i: (i, 0, 0))],
        out_specs=[pl.BlockSpec((1, n * nk, width), lambda i: (i, 0, 0)),
                   pl.BlockSpec((1, n, sw), lambda i: (i, 0, 0))],
        out_shape=[jax.ShapeDtypeStruct((q, n * nk, width), BF16),
                   jax.ShapeDtypeStruct((q, n, sw), F32)],
        scratch_shapes=[pltpu.VMEM((width, width), BF16), pltpu.VMEM((width, sw), BF16),
                        pltpu.VMEM((sw, width), BF16), pltpu.VMEM((n * nk, sw), F32)],
        compiler_params=_cp("parallel"),
        name="s5_scan",
    )(uc, kc, mc, vc, ad, d_t)
    y = yc.reshape(q, n, nk, chunk, LANE).transpose(1, 2, 3, 0, 4).reshape(n * t, dm)
    out = _mm_glu(y, w_glu, layer).reshape(n, t, dm)
    sf = sfin.reshape(q, n, 2, gl, p).transpose(2, 1, 0, 3, 4).reshape(2, n, g, p)
    return out, sf[0], sf[1]


def _s5_step_kernel(u_ref, sr_ref, si_ref, bb_ref, cc_ref, a_ref, d_ref, y_ref, or_ref, oi_ref):
    u = u_ref[...]
    bu = jnp.dot(u.astype(BF16), bb_ref[0], preferred_element_type=F32)
    half = bu.shape[1] // 2
    ar, ai = a_ref[0, :, :half], a_ref[0, :, half:]
    sr, si = sr_ref[...], si_ref[...]
    nr = bu[:, :half] + (ar * sr - ai * si)
    ni = bu[:, half:] + (ar * si + ai * sr)
    or_ref[...] = nr
    oi_ref[...] = ni
    s = jnp.concatenate([nr, ni], axis=1).astype(BF16)
    y = jnp.dot(s, cc_ref[0], preferred_element_type=F32) + d_ref[0] * u
    y_ref[...] = jax.nn.gelu(y).astype(BF16)


def _s5_sample(h, s0_re, s0_im, prm, w_in, d_skip, w_glu, layer):
    n, _, dm = h.shape
    g, p = prm[0].shape
    gl = LANE // prm[3].shape[-1]
    q = g // gl
    sw = gl * p
    u = _mm(h.reshape(n, dm), w_in, layer=layer)
    bb, cc, a = _s5_step_operators(prm)
    y, nr, ni = pl.pallas_call(
        _s5_step_kernel,
        grid=(q,),
        in_specs=[pl.BlockSpec((n, LANE), lambda i: (0, i)),
                  pl.BlockSpec((n, sw), lambda i: (0, i)),
                  pl.BlockSpec((n, sw), lambda i: (0, i)),
                  pl.BlockSpec((1, LANE, 2 * sw), lambda i: (i, 0, 0)),
                  pl.BlockSpec((1, 2 * sw, LANE), lambda i: (i, 0, 0)),
                  pl.BlockSpec((1, 1, 2 * sw), lambda i: (i, 0, 0)),
                  pl.BlockSpec((1, 1, LANE), lambda i: (i, 0, 0))],
        out_specs=[pl.BlockSpec((n, LANE), lambda i: (0, i)),
                   pl.BlockSpec((n, sw), lambda i: (0, i)),
                   pl.BlockSpec((n, sw), lambda i: (0, i))],
        out_shape=[jax.ShapeDtypeStruct((n, dm), BF16),
                   jax.ShapeDtypeStruct((n, g * p), F32),
                   jax.ShapeDtypeStruct((n, g * p), F32)],
        compiler_params=_cp("parallel"),
        name="s5_step",
    )(u, s0_re.astype(F32).reshape(n, g * p), s0_im.astype(F32).reshape(n, g * p), bb, cc, a,
      d_skip.astype(F32).reshape(q, 1, LANE))
    out = _mm_glu(y, w_glu, layer).reshape(n, 1, dm)
    return out, nr.reshape(n, g, p), ni.reshape(n, g, p)


def _tri_lower(nrows, ncols, strict):
    r = lax.broadcasted_iota(jnp.int32, (nrows, ncols), 0)
    c = lax.broadcasted_iota(jnp.int32, (nrows, ncols), 1)
    return ((c < r) if strict else (c <= r)).astype(F32)


def _cumsum_kernel(x_ref, o_ref, *, blk):
    t = x_ref.shape[1]
    tri = _tri_lower(blk, blk, strict=False)
    carry = jnp.zeros((1, x_ref.shape[2]), F32)
    for i in range(t // blk):
        xb = x_ref[0, i * blk:(i + 1) * blk, :]
        cb = jnp.dot(tri, xb, preferred_element_type=F32, precision=HI) + carry
        o_ref[0, i * blk:(i + 1) * blk, :] = cb
        carry = cb[blk - 1:blk, :]


def _cumsum_time(x):
    n, t, hh = x.shape
    blk = _pick(t, (128,))
    return pl.pallas_call(
        functools.partial(_cumsum_kernel, blk=blk),
        grid=(n,),
        in_specs=[pl.BlockSpec((1, t, hh), lambda i: (i, 0, 0))],
        out_specs=pl.BlockSpec((1, t, hh), lambda i: (i, 0, 0)),
        out_shape=jax.ShapeDtypeStruct((n, t, hh), F32),
        compiler_params=_cp("parallel"),
        name="cumsum_time",
    )(x)


def _softmax_update(s, m_prev, l_prev):
    m_new = jnp.maximum(m_prev, jnp.max(s, axis=-1, keepdims=True))
    corr = jnp.exp(m_prev - m_new)
    p = jnp.exp(s - m_new)
    return m_new, corr, p, l_prev * corr + jnp.sum(p, axis=-1, keepdims=True)


def _softmax_update2(s, m_prev, l_prev):
    m_new = jnp.maximum(m_prev, jnp.max(s, axis=-1, keepdims=True))
    corr = jnp.exp2(m_prev - m_new)
    p = jnp.exp2(s - m_new)
    return m_new, corr, p, l_prev * corr + jnp.sum(p, axis=-1, keepdims=True)


def _causal_blocks(qi, kv_step, init):
    carry = lax.fori_loop(0, qi, lambda j, c: kv_step(j, c, False), init)
    return kv_step(qi, carry, True)


def _fox_attn_kernel(q_ref, k_ref, v_ref, fq_ref, fk_ref, o_ref, *, gq, tq, scale):
    g = lax.rem(pl.program_id(1), gq)
    qi = pl.program_id(2)
    q = q_ref[0]
    lane = lax.broadcasted_iota(jnp.int32, fq_ref.shape[2:], 1)
    fq = jnp.sum(jnp.where(lane == g, fq_ref[0, 0], 0.0), axis=1, keepdims=True)
    rowpos = qi * tq + lax.broadcasted_iota(jnp.int32, (tq, 1), 0)

    def kv_step(j, carry, diagonal):
        m_prev, l_prev, acc = carry
        start = pl.multiple_of(j * tq, tq)
        s = lax.dot_general(q, k_ref[0, pl.ds(start, tq), :], NT, preferred_element_type=F32) * scale2
        s = s + (fq2 - fk_ref[0, 0, j, pl.ds(g, 1), :] * LOG2E)
        if diagonal:
            s = jnp.where(rowpos >= (j * tq + lax.broadcasted_iota(jnp.int32, (1, tq), 1)), s, -jnp.inf)
        m_new, corr, p, l_new = _softmax_update2(s, m_prev, l_prev)
        acc = acc * corr + jnp.dot(p.astype(BF16), v_ref[0, pl.ds(start, tq), :], preferred_element_type=F32)
        return m_new, l_new, acc

    scale2 = scale * LOG2E
    fq2 = fq * LOG2E
    init = (jnp.full((tq, 1), -jnp.inf, F32), jnp.zeros((tq, 1), F32), jnp.zeros((tq, v_ref.shape[2]), F32))
    _, l, acc = _causal_blocks(qi, kv_step, init)
    o_ref[0] = (acc / l).astype(o_ref.dtype)


def _fox_prompt_attention(q, kvb, f_cum, kvh, hd):
    n, t, qd = q.shape
    heads = qd // hd
    gq = heads // kvh
    tq = _pick(t, (512, 256, 128))
    nq = t // tq
    fq = f_cum.reshape(n, t, kvh, gq).transpose(0, 2, 1, 3)
    fk = f_cum.reshape(n, nq, tq, kvh, gq).transpose(0, 3, 1, 4, 2)
    return pl.pallas_call(
        functools.partial(_fox_attn_kernel, gq=gq, tq=tq, scale=hd ** -0.5),
        grid=(n, heads, nq),
        in_specs=[pl.BlockSpec((1, tq, hd), lambda b, h, i: (b, i, h)),
                  pl.BlockSpec((1, t, hd), lambda b, h, i: (b, 0, h // gq)),
                  pl.BlockSpec((1, t, hd), lambda b, h, i: (b, 0, kvh + h // gq)),
                  pl.BlockSpec((1, 1, tq, gq), lambda b, h, i: (b, h // gq, i, 0)),
                  pl.BlockSpec((1, 1, nq, gq, tq), lambda b, h, i: (b, h // gq, 0, 0, 0))],
        out_specs=pl.BlockSpec((1, tq, hd), lambda b, h, i: (b, i, h)),
        out_shape=jax.ShapeDtypeStruct((n, t, qd), BF16),
        compiler_params=_cp("parallel", "parallel", "arbitrary"),
        name="fox_attn",
    )(q, kvb, kvb, fq, fk)


def _mla_attn_kernel(qn_ref, qr_ref, kn_ref, kr_ref, v_ref, o_ref, *, tq, scale):
    qi = pl.program_id(2)
    qn = qn_ref[0]
    qr = qr_ref[0, 0]
    rowpos = qi * tq + lax.broadcasted_iota(jnp.int32, (tq, 1), 0)

    def kv_step(j, carry, diagonal):
        m_prev, l_prev, acc = carry
        start = pl.multiple_of(j * tq, tq)
        s = lax.dot_general(qn, kn_ref[0, pl.ds(start, tq), :], NT, preferred_element_type=F32)
        s = (s + lax.dot_general(qr, kr_ref[0, pl.ds(start, tq), :], NT, preferred_element_type=F32)) * scale2
        if diagonal:
            s = jnp.where(rowpos >= (j * tq + lax.broadcasted_iota(jnp.int32, (1, tq), 1)), s, -jnp.inf)
        m_new, corr, p, l_new = _softmax_update2(s, m_prev, l_prev)
        acc = acc * corr + jnp.dot(p.astype(BF16), v_ref[0, pl.ds(start, tq), :], preferred_element_type=F32)
        return m_new, l_new, acc

    scale2 = scale * LOG2E
    init = (jnp.full((tq, 1), -jnp.inf, F32), jnp.zeros((tq, 1), F32), jnp.zeros((tq, v_ref.shape[2]), F32))
    _, l, acc = _causal_blocks(qi, kv_step, init)
    o_ref[0] = (acc / l).astype(o_ref.dtype)


def _mla_prompt_attention(qn, qr, kvb, kr, heads, dn, dr, dv):
    n, t, _ = qn.shape
    tq = _pick(t, (512, 256, 128))
    nq = t // tq
    return pl.pallas_call(
        functools.partial(_mla_attn_kernel, tq=tq, scale=(dn + dr) ** -0.5),
        grid=(n, heads, nq),
        in_specs=[pl.BlockSpec((1, tq, dn), lambda b, h, i: (b, i, h)),
                  pl.BlockSpec((1, 1, tq, dr), lambda b, h, i: (b, h, i, 0)),
                  pl.BlockSpec((1, t, dn), lambda b, h, i: (b, 0, 2 * h)),
                  pl.BlockSpec((1, t, dr), lambda b, h, i: (b, 0, 0)),
                  pl.BlockSpec((1, t, dv), lambda b, h, i: (b, 0, 2 * h + 1))],
        out_specs=pl.BlockSpec((1, tq, dv), lambda b, h, i: (b, i, h)),
        out_shape=jax.ShapeDtypeStruct((n, t, heads * dv), BF16),
        compiler_params=_cp("parallel", "parallel", "arbitrary"),
        name="mla_attn",
    )(qn, qr, kvb, kr, kvb)


def _fox_decode_kernel(pt_ref, *refs, pp, kvh, gq, hd, scale):
    del pt_ref
    q_ref, knew_ref, vnew_ref, cnew_ref = refs[:4]
    k_refs = refs[4:4 + pp]
    v_refs = refs[4 + pp:4 + 2 * pp]
    f_refs = refs[4 + 2 * pp:4 + 3 * pp]
    o_ref = refs[4 + 3 * pp]
    m_ref, l_ref, acc_ref, carry_ref = refs[5 + 3 * pp:]
    step = pl.program_id(1)
    ps = f_refs[0].shape[2]

    @pl.when(step == 0)
    def _():
        m_ref[...] = jnp.full(m_ref.shape, -jnp.inf, F32)
        l_ref[...] = jnp.zeros(l_ref.shape, F32)
        acc_ref[...] = jnp.zeros(acc_ref.shape, F32)
        carry_ref[...] = jnp.zeros(carry_ref.shape, F32)

    q = q_ref[0]
    cnew = cnew_ref[0]
    later = (lax.broadcasted_iota(jnp.int32, (ps, ps), 0) > lax.broadcasted_iota(jnp.int32, (ps, ps), 1)).astype(F32)
    carry = carry_ref[...]
    hh = kvh * gq
    own = (lax.shift_right_logical(lax.broadcasted_iota(jnp.int32, (hh, kvh * ps), 0), _log2(gq))
           == lax.shift_right_logical(lax.broadcasted_iota(jnp.int32, (hh, kvh * ps), 1), _log2(ps)))

    def by_head(ref):
        return jnp.concatenate([ref[0, pl.ds(kk, ps, stride=kvh), :] for kk in range(kvh)], axis=0).astype(BF16)

    scores = []
    for i in range(pp):
        lf = f_refs[i][0]
        after = jnp.dot(lf, later, preferred_element_type=F32, precision=HI) + carry
        carry = carry + jnp.sum(lf, axis=1, keepdims=True)
        bias = cnew + after
        s = lax.dot_general(q, by_head(k_refs[i]), NT, preferred_element_type=F32) * scale
        scores.append(jnp.where(own, s + jnp.concatenate([bias] * kvh, axis=1), -jnp.inf))
    carry_ref[...] = carry
    s_all = jnp.concatenate(scores, axis=1)
    m_new, corr, p_all, l_new = _softmax_update(s_all, m_ref[...], l_ref[...])
    m_ref[...] = m_new
    l_ref[...] = l_new
    pb = p_all.astype(BF16)
    acc = acc_ref[...] * corr
    width = kvh * ps
    for i in range(pp):
        acc = acc + jnp.dot(pb[:, i * width:(i + 1) * width], by_head(v_refs[i]), preferred_element_type=F32)
    acc_ref[...] = acc

    @pl.when(step == pl.num_programs(1) - 1)
    def _():
        qf = q.astype(F32)
        kn = knew_ref[0].astype(BF16).astype(F32)
        vn = vnew_ref[0].astype(BF16).astype(F32)
        kx = jnp.concatenate([jnp.broadcast_to(kn[kk:kk + 1], (gq, hd)) for kk in range(kvh)], axis=0)
        vx = jnp.concatenate([jnp.broadcast_to(vn[kk:kk + 1], (gq, hd)) for kk in range(kvh)], axis=0)
        s_new = jnp.sum(qf * kx, axis=1, keepdims=True) * scale + (cnew - cnew)
        m_fin, corr_fin, p_new, l_fin = _softmax_update(s_new, m_ref[...], l_ref[...])
        acc = acc_ref[...] * corr_fin + p_new.astype(BF16).astype(F32) * vx
        o_ref[0] = acc / l_fin


def _fox_sample_attention(q, k_new, v_new, logf_new, cache_k, cache_v, cache_logf, layer, page_table):
    n = q.shape[0]
    _, pool, ps, kvh, hd = cache_k.shape
    hh = cache_logf.shape[-1]
    gq = hh // kvh
    npg = page_table.shape[1]
    pp = _pick(npg, (32, 16, 8, 4, 2, 1))
    ck = cache_k.reshape(-1, ps * kvh, hd)
    cv = cache_v.reshape(-1, ps * kvh, hd)
    cf = cache_logf.reshape(-1, ps, hh).transpose(0, 2, 1)
    base = layer * pool

    def page_map(i):
        return lambda s, p, pt: (base + pt[s, npg - 1 - (p * pp + i)], 0, 0)

    seq3 = lambda s, p, pt: (s, 0, 0)
    in_specs = [pl.BlockSpec((1, hh, hd), seq3),
                pl.BlockSpec((1, kvh, hd), seq3),
                pl.BlockSpec((1, kvh, hd), seq3),
                pl.BlockSpec((1, hh, 1), seq3)]
    in_specs += [pl.BlockSpec((1, ps * kvh, hd), page_map(i)) for i in range(pp)]
    in_specs += [pl.BlockSpec((1, ps * kvh, hd), page_map(i)) for i in range(pp)]
    in_specs += [pl.BlockSpec((1, hh, ps), page_map(i)) for i in range(pp)]
    out = pl.pallas_call(
        functools.partial(_fox_decode_kernel, pp=pp, kvh=kvh, gq=gq, hd=hd, scale=hd ** -0.5),
        grid_spec=pltpu.PrefetchScalarGridSpec(
            num_scalar_prefetch=1,
            grid=(n, npg // pp),
            in_specs=in_specs,
            out_specs=pl.BlockSpec((1, hh, hd), seq3),
            scratch_shapes=[pltpu.VMEM((hh, 1), F32), pltpu.VMEM((hh, 1), F32),
                            pltpu.VMEM((hh, hd), F32), pltpu.VMEM((hh, 1), F32)]),
        out_shape=jax.ShapeDtypeStruct((n, hh, hd), F32),
        compiler_params=_cp("parallel", "arbitrary"),
        name="fox_decode",
    )(page_table, q.reshape(n, hh, hd).astype(BF16), k_new.reshape(n, kvh, hd), v_new.reshape(n, kvh, hd),
      logf_new.reshape(n, hh, 1), *([ck] * pp), *([cv] * pp), *([cf] * pp))
    return out.reshape(n, hh * hd).astype(BF16)


def _mla_decode_kernel(pt_ref, *refs, pp, scale):
    del pt_ref
    ql_ref, qr_ref, cnew_ref, rnew_ref = refs[:4]
    c_refs = refs[4:4 + pp]
    r_refs = refs[4 + pp:4 + 2 * pp]
    o_ref = refs[4 + 2 * pp]
    m_ref, l_ref, acc_ref = refs[5 + 2 * pp:]
    step = pl.program_id(1)
    ps = c_refs[0].shape[1]

    @pl.when(step == 0)
    def _():
        m_ref[...] = jnp.full(m_ref.shape, -jnp.inf, F32)
        l_ref[...] = jnp.zeros(l_ref.shape, F32)
        acc_ref[...] = jnp.zeros(acc_ref.shape, F32)

    ql = ql_ref[0]
    qr = qr_ref[0]
    cbs, scores = [], []
    for i in range(pp):
        cb = c_refs[i][0].astype(BF16)
        rb = r_refs[i][0].astype(BF16)
        s = lax.dot_general(ql, cb, NT, preferred_element_type=F32)
        scores.append((s + lax.dot_general(qr, rb, NT, preferred_element_type=F32)) * scale)
        cbs.append(cb)
    s_all = jnp.concatenate(scores, axis=1)
    m_new, corr, p_all, l_new = _softmax_update(s_all, m_ref[...], l_ref[...])
    m_ref[...] = m_new
    l_ref[...] = l_new
    pb = p_all.astype(BF16)
    acc = acc_ref[...] * corr
    for i in range(pp):
        acc = acc + jnp.dot(pb[:, i * ps:(i + 1) * ps], cbs[i], preferred_element_type=F32)
    acc_ref[...] = acc

    @pl.when(step == pl.num_programs(1) - 1)
    def _():
        cn = cnew_ref[0].astype(BF16).astype(F32)
        rn = rnew_ref[0].astype(BF16).astype(F32)
        s_new = (jnp.sum(ql.astype(F32) * cn, axis=1, keepdims=True)
                 + jnp.sum(qr.astype(F32) * rn, axis=1, keepdims=True)) * scale
        m_fin, corr_fin, p_new, l_fin = _softmax_update(s_new, m_ref[...], l_ref[...])
        acc_fin = acc_ref[...] * corr_fin + p_new.astype(BF16).astype(F32) * cn
        o_ref[0] = (acc_fin / l_fin).astype(o_ref.dtype)


def _mla_sample_attention(q_lat, q_rot, ckv_new, kpe_new, cache_ckv, cache_kpe, layer, page_table, heads, scale):
    n = q_lat.shape[0]
    _, pool, ps, kl = cache_ckv.shape
    rr = cache_kpe.shape[-1]
    npg = page_table.shape[1]
    pp = _pick(npg, (32, 16, 8, 4, 2, 1))
    cc = cache_ckv.reshape(-1, ps, kl)
    cr = cache_kpe.reshape(-1, ps, rr)
    base = layer * pool

    def page_map(i):
        return lambda s, p, pt: (base + pt[s, p * pp + i], 0, 0)

    seq3 = lambda s, p, pt: (s, 0, 0)
    in_specs = [pl.BlockSpec((1, heads, kl), seq3), pl.BlockSpec((1, heads, rr), seq3),
                pl.BlockSpec((1, 1, kl), seq3), pl.BlockSpec((1, 1, rr), seq3)]
    in_specs += [pl.BlockSpec((1, ps, kl), page_map(i)) for i in range(pp)]
    in_specs += [pl.BlockSpec((1, ps, rr), page_map(i)) for i in range(pp)]
    out = pl.pallas_call(
        functools.partial(_mla_decode_kernel, pp=pp, scale=scale),
        grid_spec=pltpu.PrefetchScalarGridSpec(
            num_scalar_prefetch=1,
            grid=(n, npg // pp),
            in_specs=in_specs,
            out_specs=pl.BlockSpec((1, heads, kl), seq3),
            scratch_shapes=[pltpu.VMEM((heads, 1), F32), pltpu.VMEM((heads, 1), F32),
                            pltpu.VMEM((heads, kl), F32)]),
        out_shape=jax.ShapeDtypeStruct((n, heads, kl), BF16),
        compiler_params=_cp("parallel", "arbitrary"),
        name="mla_decode",
    )(page_table, q_lat.reshape(n, heads, kl), q_rot.reshape(n, heads, rr),
      ckv_new.reshape(n, 1, kl), kpe_new.reshape(n, 1, rr), *([cc] * pp), *([cr] * pp))
    return out.reshape(n, heads * kl)


def _mla_norm_kernel(z_ref, gq_ref, gkv_ref, cq_ref, ckv_ref, ckvb_ref, *, ql):
    z = z_ref[...]
    zq, zk = z[:, :ql], z[:, ql:]
    cq = zq * lax.rsqrt(jnp.mean(zq * zq, axis=-1, keepdims=True) + RMS_EPS) * gq_ref[...]
    ck = zk * lax.rsqrt(jnp.mean(zk * zk, axis=-1, keepdims=True) + RMS_EPS) * gkv_ref[...]
    cq_ref[...] = cq.astype(BF16)
    ckv_ref[...] = ck
    ckvb_ref[...] = ck.astype(BF16)


def _mla_norm(z, g_q, g_kv):
    m, w = z.shape
    ql = g_q.shape[0]
    kl = g_kv.shape[0]
    tm = _pick(m, (512, 256, 128))
    return pl.pallas_call(
        functools.partial(_mla_norm_kernel, ql=ql),
        grid=(m // tm,),
        in_specs=[pl.BlockSpec((tm, w), lambda i: (i, 0)),
                  pl.BlockSpec((1, ql), lambda i: (0, 0)),
                  pl.BlockSpec((1, kl), lambda i: (0, 0))],
        out_specs=[pl.BlockSpec((tm, ql), lambda i: (i, 0)),
                   pl.BlockSpec((tm, kl), lambda i: (i, 0)),
                   pl.BlockSpec((tm, kl), lambda i: (i, 0))],
        out_shape=[jax.ShapeDtypeStruct((m, ql), BF16), jax.ShapeDtypeStruct((m, kl), F32),
                   jax.ShapeDtypeStruct((m, kl), BF16)],
        compiler_params=_cp("parallel"),
        name="mla_norm",
    )(z, g_q.astype(F32).reshape(1, ql), g_kv.astype(F32).reshape(1, kl))


def _rope_tables(pos, half, reps):
    inv = ROPE_BASE ** (-jnp.arange(half, dtype=F32) / half)
    ang = pos[:, None] * inv
    cos, sin = jnp.cos(ang), jnp.sin(ang)
    return (jnp.tile(jnp.concatenate([cos, cos], axis=-1), (1, reps)),
            jnp.tile(jnp.concatenate([-sin, sin], axis=-1), (1, reps)))


def _swap_halves(w, width):
    k, n = w.shape
    return w.reshape(k, n // width, 2, width // 2)[:, :, ::-1, :].reshape(k, n)


def _sort_desc(v):
    v = list(v)
    n = len(v)
    k = 2
    while k <= n:
        j = k // 2
        while j >= 1:
            for i in range(n):
                l = i ^ j
                if l > i:
                    hi, lo = jnp.maximum(v[i], v[l]), jnp.minimum(v[i], v[l])
                    v[i], v[l] = (hi, lo) if (i & k) == 0 else (lo, hi)
            j //= 2
        k *= 2
    return v


def _merge_desc(v):
    v = list(v)
    j = len(v) // 2
    while j >= 1:
        for i in range(len(v)):
            l = i ^ j
            if l > i:
                v[i], v[l] = jnp.maximum(v[i], v[l]), jnp.minimum(v[i], v[l])
        j //= 2
    return v


def _top_of_two(a, b):
    n = len(a)
    return _merge_desc([jnp.maximum(a[i], b[n - 1 - i]) for i in range(n)])


def _fold_sublanes(v):
    shift = v[0].shape[0] // 2
    while shift >= 1:
        v = _top_of_two(v, [pltpu.roll(x, shift, 0) for x in v])
        shift //= 2
    return v


def _peer_route_kernel(qp_ref, sk_ref, s1_ref, c1_ref, s2_ref, e2_ref, tau_ref, *, heads, nkeys, topk):
    tt = qp_ref.shape[0]
    sub = lax.broadcasted_iota(jnp.int32, (SUBLANE, tt), 0)
    for h in range(heads):
        sts, vals = [], []
        for c in range(2):
            qh = qp_ref[:, (2 * h + c) * nkeys:(2 * h + c + 1) * nkeys]
            st = lax.dot_general(sk_ref[c, h], qh, NT, preferred_element_type=F32, precision=HI)
            sts.append(st)
            slabs = [st[SUBLANE * i:SUBLANE * (i + 1), :] for i in range(nkeys // SUBLANE)]
            vals.append(_fold_sublanes(_sort_desc(slabs)))
        v1, v2 = vals
        cands = []
        for half in range(topk // SUBLANE):
            b = v2[half * SUBLANE + SUBLANE - 1]
            for s in range(SUBLANE - 2, -1, -1):
                b = jnp.where(sub == s, v2[half * SUBLANE + s], b)
            cands.append([a + b for a in v1])
        top = cands[0]
        for other in cands[1:]:
            top = _top_of_two(top, other)
        top = _fold_sublanes(top)
        mx = top[0][0:1]
        z = jnp.zeros_like(mx)
        for r in range(topk):
            z = z + jnp.exp(top[r][0:1] - mx)
        s1_ref[h] = sts[0]
        s2_ref[h] = sts[1]
        c1_ref[h] = jnp.exp(sts[0] - v1[0][0:1]) / z
        e2_ref[h] = jnp.exp(sts[1] - v2[0][0:1])
        tau_ref[h:h + 1, :] = top[topk - 1][0:1]


def _peer_dense_kernel(h_ref, u_ref, v_ref, s1_ref, c1_ref, s2_ref, e2_ref, tau_ref, o_ref, *rest,
                       heads, nkeys, emit_bf16):
    ht_ref = rest[-1]
    e = pl.program_id(1)
    eb = u_ref.shape[0]
    tt = h_ref.shape[0]
    per = eb // nkeys

    @pl.when(e == 0)
    def _():
        ht_ref[...] = h_ref[...].astype(F32).T.astype(BF16)
        o_ref[...] = jnp.zeros(o_ref.shape, F32)

    u, v = u_ref[...].astype(BF16), v_ref[...].astype(BF16)
    if emit_bf16:
        rest[0][...] = u
        rest[1][...] = v
    st = jnp.dot(u, ht_ref[...], preferred_element_type=F32)
    act = jax.nn.gelu(st)
    parts = []
    for al in range(per):
        a = e * per + al
        w = jnp.zeros((nkeys, tt), F32)
        for h in range(heads):
            sc = s2_ref[h] + s1_ref[h, pl.ds(a, 1), :]
            w = w + jnp.where(sc >= tau_ref[h:h + 1, :], e2_ref[h] * c1_ref[h, pl.ds(a, 1), :], 0.0)
        parts.append(w * act[al * nkeys:(al + 1) * nkeys])
    pt = parts[0] if len(parts) == 1 else jnp.concatenate(parts, axis=0)
    o_ref[...] += jnp.dot(pt.T.astype(BF16), v, preferred_element_type=F32)


def _peer_ffn(h, w_q, subkeys, u_tab, v_tab, layer, emit_bf16=False):
    m, d = h.shape
    _, heads, nkeys, kd2 = subkeys.shape
    assert nkeys == LANE and kd2 == LANE, "product-key halves are laid out on one 128-lane tile"
    assert nkeys // SUBLANE == PEER_TOPK, "the top-k network keeps one value per slab of keys"
    ne = u_tab.shape[-2]
    qp = _mm(h, w_q, layer=layer)
    tt = _pick(m, (256, 128))
    tab = jax.ShapeDtypeStruct((heads, nkeys, m), F32)
    tspec = pl.BlockSpec((heads, nkeys, tt), lambda i: (0, 0, i))
    s1, c1, s2, e2, tau = pl.pallas_call(
        functools.partial(_peer_route_kernel, heads=heads, nkeys=nkeys, topk=PEER_TOPK),
        grid=(m // tt,),
        in_specs=[pl.BlockSpec((tt, qp.shape[1]), lambda i: (i, 0)),
                  pl.BlockSpec((2, heads, nkeys, kd2), lambda i: (0, 0, 0, 0))],
        out_specs=[tspec, tspec, tspec, tspec, pl.BlockSpec((heads, tt), lambda i: (0, i))],
        out_shape=[tab, tab, tab, tab, jax.ShapeDtypeStruct((heads, m), F32)],
        compiler_params=_cp("parallel"),
        name="peer_route",
    )(qp, subkeys.astype(F32))
    td = _pick(m, (512, 256, 128))
    eb = _pick(ne, (256, 128)) if emit_bf16 else _pick(ne, (512, 256, 128))
    once = pl.Buffered(1)
    dspec = pl.BlockSpec((heads, nkeys, td), lambda i, e: (0, 0, i), pipeline_mode=once)
    ospec = pl.BlockSpec((td, d), lambda i, e: (i, 0))
    oshape = jax.ShapeDtypeStruct((m, d), F32)
    if emit_bf16:
        assert m == td, "every expert block is cast and written exactly once"
        wspec = pl.BlockSpec((None, eb, d), lambda i, e: (layer, e, 0))
        bspec = pl.BlockSpec((eb, d), lambda i, e: (e, 0))
        bshape = jax.ShapeDtypeStruct((ne, d), BF16)
        ospec, oshape = [ospec, bspec, bspec], [oshape, bshape, bshape]
    else:
        wspec = pl.BlockSpec((eb, d), lambda i, e: (e, 0))
    return pl.pallas_call(
        functools.partial(_peer_dense_kernel, heads=heads, nkeys=nkeys, emit_bf16=emit_bf16),
        grid=(m // td, ne // eb),
        in_specs=[pl.BlockSpec((td, d), lambda i, e: (i, 0), pipeline_mode=once),
                  wspec, wspec, dspec, dspec, dspec, dspec,
                  pl.BlockSpec((heads, td), lambda i, e: (0, i), pipeline_mode=once)],
        out_specs=ospec,
        out_shape=oshape,
        scratch_shapes=[pltpu.VMEM((d, td), BF16)],
        compiler_params=_cp("parallel", "arbitrary"),
        name="peer_dense",
    )(h, u_tab, v_tab, s1, c1, s2, e2, tau)


def kernel(x_prompt, x_sample, state_s5_re, state_s5_im, cache_fox_k, cache_fox_v, cache_fox_logf, cache_mla_ckv, cache_mla_kpe, page_table, c_prompt, c_sample, ada_w, ada_b, ln_g, ln_b, s5_w_in, s5_lam_re, s5_lam_im, s5_log_step, s5_b_re, s5_b_im, s5_c_re, s5_c_im, s5_d, s5_w_glu, fox_w_in, fox_b_f, fox_w_o, mla_w_in, mla_q_norm, mla_kv_norm, mla_w_uq, mla_w_ukv, mla_w_o, peer_w_q, peer_subkeys, peer_u, peer_v):
    n_p, t_p, dm = x_prompt.shape
    n_s, t_s, _ = x_sample.shape
    assert t_s == 1, "the sample group decodes one token per sequence"
    depth = ada_w.shape[0]
    alpha = (2 * depth) ** 0.25
    past_len = page_table.shape[1] * cache_fox_k.shape[2]
    fox_heads = fox_b_f.shape[-1]
    fox_kvh, fox_hd = cache_fox_k.shape[3], cache_fox_k.shape[4]
    mla_ql, mla_kl, mla_r = mla_q_norm.shape[-1], mla_kv_norm.shape[-1], cache_mla_kpe.shape[-1]
    mla_heads = (mla_w_uq.shape[-1] - mla_w_ukv.shape[-1] + mla_w_o.shape[-2]) // mla_r
    mla_dv = mla_w_o.shape[-2] // mla_heads
    mla_dn = mla_w_ukv.shape[-1] // mla_heads - mla_dv
    assert mla_dn == mla_dv, "k_nope and v column blocks of w_ukv are addressed with one block width"

    s5_w_in_b, s5_w_glu_b = s5_w_in.astype(BF16), s5_w_glu.astype(BF16)
    fox_w_in_b, fox_w_o_b = fox_w_in.astype(BF16), fox_w_o.astype(BF16)
    mla_w_in_b, mla_w_uq_b = mla_w_in.astype(BF16), mla_w_uq.astype(BF16)
    mla_w_ukv_b, mla_w_o_b = mla_w_ukv.astype(BF16), mla_w_o.astype(BF16)
    peer_w_q_b = peer_w_q.astype(BF16)

    rows = n_p + n_s
    rpad = -(-rows // SUBLANE) * SUBLANE
    c_all = jnp.pad(jnp.concatenate([c_prompt, c_sample], axis=0).astype(F32), ((0, rpad - rows), (0, 0)))
    mod = _modulation_all(c_all, ada_w.astype(F32), ada_b.astype(F32))

    def mods(i, s):
        m = mod[2 * i + s]
        mp = m[:n_p, None, :]
        ms = m[n_p:rows][None]
        cut = lambda a: (a[..., :dm], a[..., dm:2 * dm], a[..., 2 * dm:])
        return cut(mp), cut(ms)

    pos_p = jnp.arange(t_p, dtype=F32)
    pos_s = past_len + jnp.arange(t_s, dtype=F32)
    cos_p, sin_p = _rope_tables(pos_p, mla_r // 2, LANE // mla_r)
    cos_p, sin_p = jnp.tile(cos_p, (n_p, 1)), jnp.tile(sin_p, (n_p, 1))
    cos_s, sin_s = _rope_tables(jnp.tile(pos_s, (n_s,)), mla_r // 2, LANE // mla_r)

    yp, ys = x_prompt.astype(F32), x_sample.astype(F32).reshape(1, n_s, dm)
    (shp, scp, _), (shs, scs, _) = mods(0, 0)
    hp = _ln_mod(yp, scale=scp, shift=shp, alpha=alpha)
    hs = _ln_mod(ys, scale=scs, shift=shs, alpha=alpha)

    out = {k: [] for k in ("s5r_p", "s5i_p", "s5r_s", "s5i_s", "fk_p", "fv_p", "fl_p", "fk_s", "fv_s", "fl_s",
                           "ckv_p", "kpe_p", "ckv_s", "kpe_s")}
    for i in range(depth):
        kind, j = i % 3, i // 3
        (_, _, gate_p), (_, _, gate_s) = mods(i, 0)
        if kind == 0:
            prm = (s5_lam_re[j], s5_lam_im[j], s5_log_step[j], s5_b_re[j], s5_b_im[j], s5_c_re[j], s5_c_im[j])
            op, sr, si = _s5_prompt(hp, prm, s5_w_in_b, s5_d[j], s5_w_glu_b, j)
            out["s5r_p"].append(sr)
            out["s5i_p"].append(si)
            os_, sr, si = _s5_sample(hs.reshape(n_s, 1, dm), state_s5_re[j], state_s5_im[j], prm, s5_w_in_b,
                                     s5_d[j], s5_w_glu_b, j)
            out["s5r_s"].append(sr)
            out["s5i_s"].append(si)
            os_ = os_.reshape(1, n_s, dm)
        elif kind == 1:
            nq, nkv = fox_heads * fox_hd, fox_kvh * fox_hd
            w_f = fox_w_in_b[j][:, nq + 2 * nkv:]
            h2 = hp.reshape(n_p * t_p, dm)
            q = _mm(h2, fox_w_in_b, (BF16,), layer=j, col0=0, ncols=nq)
            kv, kvb = _mm(h2, fox_w_in_b, (F32, BF16), layer=j, col0=nq, ncols=2 * nkv)
            lf = _mm_logsig(h2, w_f, fox_b_f[j])
            f_cum = _cumsum_time(lf.reshape(n_p, t_p, fox_heads))
            o = _fox_prompt_attention(q.reshape(n_p, t_p, nq), kvb.reshape(n_p, t_p, 2 * nkv), f_cum,
                                      fox_kvh, fox_hd)
            op = _mm(o.reshape(n_p * t_p, nq), fox_w_o_b, layer=j).reshape(n_p, t_p, dm)
            out["fk_p"].append(kv[:, :nkv].reshape(n_p, t_p, fox_kvh, fox_hd))
            out["fv_p"].append(kv[:, nkv:].reshape(n_p, t_p, fox_kvh, fox_hd))
            out["fl_p"].append(lf.reshape(n_p, t_p, fox_heads))
            h2 = hs.reshape(n_s, dm)
            q = _mm(h2, fox_w_in_b, layer=j, col0=0, ncols=nq)
            kv = _mm(h2, fox_w_in_b, layer=j, col0=nq, ncols=2 * nkv)
            lf = _mm_logsig(h2, w_f, fox_b_f[j])
            o = _fox_sample_attention(q, kv[:, :nkv], kv[:, nkv:], lf, cache_fox_k, cache_fox_v,
                                      cache_fox_logf, j, page_table)
            os_ = _mm(o, fox_w_o_b, layer=j).reshape(1, n_s, dm)
            out["fk_s"].append(kv[:, :nkv].reshape(n_s, 1, fox_kvh, fox_hd))
            out["fv_s"].append(kv[:, nkv:].reshape(n_s, 1, fox_kvh, fox_hd))
            out["fl_s"].append(lf.reshape(n_s, 1, fox_heads))
        else:
            nlat = mla_ql + mla_kl
            w_pe = mla_w_in_b[j][:, nlat:]
            w_pe_rot = _swap_halves(w_pe, mla_r)
            wq = mla_w_uq_b[j].reshape(mla_ql, mla_heads, mla_dn + mla_r)
            wq_n = wq[:, :, :mla_dn].reshape(mla_ql, mla_heads * mla_dn)
            wq_r = wq[:, :, mla_dn:].reshape(mla_ql, mla_heads * mla_r)
            wq_r_rot = _swap_halves(wq_r, mla_r)

            def project(h2, cos, sin):
                z = _mm(h2, mla_w_in_b, layer=j, col0=0, ncols=nlat)
                cq, ckv, ckvb = _mla_norm(z, mla_q_norm[j], mla_kv_norm[j])
                kpe = _mm_rope(h2, w_pe, w_pe_rot, cos[:, :mla_r], sin[:, :mla_r], (F32,))
                qn = _mm(cq, wq_n, (BF16,))
                qr = _mm_rope(cq, wq_r, wq_r_rot, cos, sin, (BF16,))
                return qn, qr, ckv, ckvb, kpe

            qn, qr, ckv, ckvb, kpe = project(hp.reshape(n_p * t_p, dm), cos_p, sin_p)
            kvb = _mm(ckvb, mla_w_ukv_b, (BF16,), layer=j)
            qr4 = qr.reshape(n_p, t_p, mla_heads, mla_r).transpose(0, 2, 1, 3)
            o = _mla_prompt_attention(qn.reshape(n_p, t_p, -1), qr4, kvb.reshape(n_p, t_p, -1),
                                      kpe.astype(BF16).reshape(n_p, t_p, mla_r), mla_heads, mla_dn, mla_r, mla_dv)
            op = _mm(o.reshape(n_p * t_p, -1), mla_w_o_b, layer=j).reshape(n_p, t_p, dm)
            out["ckv_p"].append(ckv.reshape(n_p, t_p, mla_kl))
            out["kpe_p"].append(kpe.reshape(n_p, t_p, mla_r))
            qn, qr, ckv, ckvb, kpe = project(hs.reshape(n_s, dm), cos_s, sin_s)
            q_lat = _head_mm(qn, mla_w_ukv_b, j, mla_heads, 0, True, BF16)
            a = _mla_sample_attention(q_lat, qr, ckv, kpe, cache_mla_ckv, cache_mla_kpe, j, page_table, mla_heads,
                                      (mla_dn + mla_r) ** -0.5)
            o = _head_mm(a, mla_w_ukv_b, j, mla_heads, 1, False, BF16)
            os_ = _mm(o, mla_w_o_b, layer=j).reshape(1, n_s, dm)
            out["ckv_s"].append(ckv.reshape(n_s, 1, mla_kl))
            out["kpe_s"].append(kpe.reshape(n_s, 1, mla_r))
        (shp, scp, gate2_p), (shs, scs, gate2_s) = mods(i, 1)
        yp, hp = _ln_mod(yp, op, gate_p, ln_g[i, 0], ln_b[i, 0], scp, shp, alpha=alpha)
        ys, hs = _ln_mod(ys, os_, gate_s, ln_g[i, 0], ln_b[i, 0], scs, shs, alpha=alpha)
        fs, u_b, v_b = _peer_ffn(hs.reshape(n_s, dm), peer_w_q_b, peer_subkeys[i], peer_u.astype(F32),
                                 peer_v.astype(F32), i, emit_bf16=True)
        fp = _peer_ffn(hp.reshape(n_p * t_p, dm), peer_w_q_b, peer_subkeys[i], u_b, v_b, i)
        fp, fs = fp.reshape(n_p, t_p, dm), fs.reshape(1, n_s, dm)
        if i + 1 < depth:
            (shp, scp, _), (shs, scs, _) = mods(i + 1, 0)
            yp, hp = _ln_mod(yp, fp, gate2_p, ln_g[i, 1], ln_b[i, 1], scp, shp, alpha=alpha)
            ys, hs = _ln_mod(ys, fs, gate2_s, ln_g[i, 1], ln_b[i, 1], scs, shs, alpha=alpha)
        else:
            yp = _ln_mod(yp, fp, gate2_p, ln_g[i, 1], ln_b[i, 1], alpha=alpha)
            ys = _ln_mod(ys, fs, gate2_s, ln_g[i, 1], ln_b[i, 1], alpha=alpha)

    st = lambda k: jnp.stack(out[k])
    return (yp, ys.reshape(n_s, 1, dm),
            st("s5r_p"), st("s5i_p"), st("fk_p"), st("fv_p"), st("fl_p"), st("ckv_p"), st("kpe_p"),
            st("s5r_s"), st("s5i_s"), st("fk_s"), st("fv_s"), st("fl_s"), st("ckv_s"), st("kpe_s"))
```
